```python
import jax, jax.numpy as jnp
from jax import lax
import numpy as np

D_MODEL = 2048
BATCH = 4
SEQ = 8192
DEPTH = 2

CHUNK = 64
Q_BLOCK = 128
PLE_DIM = 256
HEAD_DIM = 128
N_HEADS_FOX = D_MODEL // HEAD_DIM
EXPAND = 128
N_HEADS_HG = D_MODEL // EXPAND
HG_V_DIM = D_MODEL // N_HEADS_HG
EPS = 1e-6
NEG = -1e30

kernel_name = "fox_hgrn2_interleaved_sandwich_ple"


def rms_norm(x, g):
    xf = x.astype(jnp.float32)
    y = xf * lax.rsqrt(jnp.mean(xf * xf, axis=-1, keepdims=True) + EPS)
    return (y * g.astype(jnp.float32)).astype(x.dtype)


def fox_mixer(xn, w_in, b_f, g_q, g_k, w_out):
    B, S, D = xn.shape
    H, Dh = N_HEADS_FOX, HEAD_DIM
    proj = xn @ w_in
    q, k, v, z, f_logit = jnp.split(proj, [D, 2 * D, 3 * D, 4 * D], axis=-1)
    q = rms_norm(q.reshape(B, S, H, Dh), g_q).transpose(0, 2, 1, 3)
    k = rms_norm(k.reshape(B, S, H, Dh), g_k).transpose(0, 2, 1, 3)
    v = v.reshape(B, S, H, Dh).transpose(0, 2, 1, 3)
    log_f = jax.nn.log_sigmoid((f_logit + b_f).astype(jnp.float32))
    c = jnp.cumsum(log_f, axis=1).transpose(0, 2, 1)
    nb = S // Q_BLOCK
    q_blocks = q.reshape(B, H, nb, Q_BLOCK, Dh).transpose(2, 0, 1, 3, 4)
    c_blocks = c.reshape(B, H, nb, Q_BLOCK).transpose(2, 0, 1, 3)
    kpos = jnp.arange(S)
    scale = 1.0 / np.sqrt(Dh).astype(np.float32)

    def attend(args):
        qb, cqb, bid = args
        qpos = bid * Q_BLOCK + jnp.arange(Q_BLOCK)
        s = jnp.einsum('bhqd,bhkd->bhqk', qb, k).astype(jnp.float32) * scale
        s = s + (cqb[..., :, None] - c[:, :, None, :])
        mask = kpos[None, :] <= qpos[:, None]
        s = jnp.where(mask, s, NEG)
        prob = jax.nn.softmax(s, axis=-1)
        return jnp.einsum('bhqk,bhkd->bhqd', prob.astype(v.dtype), v).astype(v.dtype)

    o = lax.map(attend, (q_blocks, c_blocks, jnp.arange(nb)))
    o = o.transpose(1, 0, 3, 2, 4).reshape(B, S, D)
    return (o * jax.nn.silu(z)) @ w_out


def hgrn2_mixer(xn, w_in, lb, g_o, w_out):
    B, S, D = xn.shape
    H, dk, dv = N_HEADS_HG, EXPAND, HG_V_DIM
    nc = S // CHUNK
    proj = xn @ w_in
    q, f_logit, i_in, z = jnp.split(proj, 4, axis=-1)
    q = jax.nn.silu(q.astype(jnp.float32))
    forget = lb + (1.0 - lb) * jax.nn.sigmoid(f_logit.astype(jnp.float32))
    k = 1.0 - forget
    g = jnp.log(forget)

    def to_chunks(t, hd):
        return t.reshape(B, nc, CHUNK, H, hd).transpose(1, 0, 3, 2, 4)

    qc, kc, gc = to_chunks(q, dk), to_chunks(k, dk), to_chunks(g, dk)
    vc = to_chunks(i_in.astype(jnp.float32), dv)
    causal = jnp.tril(jnp.ones((CHUNK, CHUNK), dtype=bool))

    def step(state, inp):
        qq, kk, vv, gg = inp
        b = jnp.cumsum(gg, axis=2)
        inter = jnp.einsum('bhtd,bhdv->bhtv', qq * jnp.exp(b), state)
        diff = b[:, :, :, None, :] - b[:, :, None, :, :]
        decay = jnp.exp(jnp.where(causal[:, :, None], diff, -jnp.inf))
        att = jnp.einsum('bhtd,bhsd,bhtsd->bhts', qq, kk, decay)
        intra = jnp.einsum('bhts,bhsv->bhtv', att, vv)
        b_last = b[:, :, -1:, :]
        new_state = jnp.exp(b_last[:, :, 0, :])[..., None] * state + jnp.einsum(
            'bhsd,bhsv->bhdv', kk * jnp.exp(b_last - b), vv)
        return new_state, inter + intra

    s0 = jnp.zeros((B, H, dk, dv), jnp.float32)
    _, o = lax.scan(step, s0, (qc, kc, vc, gc))
    o = o.transpose(1, 0, 3, 2, 4).reshape(B, S, H, dv)
    o = rms_norm(o, g_o).reshape(B, S, D).astype(xn.dtype)
    return (o * jax.nn.silu(z)) @ w_out


def setup_inputs(seed: int = 0) -> dict:
    key = jax.random.key(seed)
    ks = jax.random.split(key, 16)
    D, H = D_MODEL, N_HEADS_FOX
    n_fox = (DEPTH + 1) // 2
    n_hg = DEPTH // 2
    nrm = lambda k, shape, fan: jax.random.normal(k, shape, jnp.float32) * fan ** -0.5
    gain = lambda k, shape: 1.0 + 0.05 * jax.random.normal(k, shape, jnp.float32)
    return {
        "x": jax.random.normal(ks[0], (BATCH, SEQ, D), jnp.float32),
        "p": jax.random.normal(ks[1], (DEPTH, BATCH, SEQ, PLE_DIM), jnp.float32),
        "w_in_fox": nrm(ks[2], (n_fox, D, 4 * D + H), D),
        "b_f_fox": jax.random.uniform(ks[3], (n_fox, H), jnp.float32, 1.0, 4.0),
        "g_q_fox": gain(ks[4], (n_fox, HEAD_DIM)),
        "g_k_fox": gain(ks[5], (n_fox, HEAD_DIM)),
        "w_out_fox": nrm(ks[6], (n_fox, D, D), D),
        "w_in_hg": nrm(ks[7], (n_hg, D, 4 * D), D),
        "lb_logits": 0.1 * jax.random.normal(ks[8], (DEPTH, D), jnp.float32),
        "g_o_hg": gain(ks[9], (n_hg, HG_V_DIM)),
        "w_out_hg": nrm(ks[10], (n_hg, D, D), D),
        "pre_norm": gain(ks[11], (DEPTH, D)),
        "post_norm": gain(ks[12], (DEPTH, D)),
        "w_pe": nrm(ks[13], (DEPTH, PLE_DIM, D), PLE_DIM),
        "w_pg": nrm(ks[14], (DEPTH, D, D), D),
    }


def reference(x, p, w_in_fox, b_f_fox, g_q_fox, g_k_fox, w_out_fox, w_in_hg,
              lb_logits, g_o_hg, w_out_hg, pre_norm, post_norm, w_pe, w_pg):
    gam = jax.nn.softmax(lb_logits.astype(jnp.float32), axis=0)
    lower_bounds = jnp.cumsum(gam, axis=0) - gam[0:1]
    h = x
    for i in range(DEPTH):
        xn = rms_norm(h, pre_norm[i])
        if i % 2 == 0:
            j = i // 2
            y = fox_mixer(xn, w_in_fox[j], b_f_fox[j], g_q_fox[j], g_k_fox[j], w_out_fox[j])
        else:
            j = i // 2
            y = hgrn2_mixer(xn, w_in_hg[j], lower_bounds[i], g_o_hg[j], w_out_hg[j])
        h = h + rms_norm(y, post_norm[i])
        pe = p[i].astype(h.dtype) @ w_pe[i]
        h = h + pe * jax.nn.sigmoid(h @ w_pg[i])
    return h
```

```python
import functools
import math

import jax
import jax.numpy as jnp
from jax import lax
from jax.experimental import pallas as pl
from jax.experimental.pallas import tpu as pltpu

HEAD_DIM = 128
EPS = 1e-6
NEG = -1e30
HG_CHUNK = 64
HG_LEVELS = (1, 2, 4, 8, 16, 32)

F32 = jnp.float32
BF16 = jnp.bfloat16

VMEM_LIMIT_BYTES = 56 * 1024 * 1024


def _cparams(semantics):
    return pltpu.CompilerParams(dimension_semantics=semantics,
                                vmem_limit_bytes=VMEM_LIMIT_BYTES)


def _sigmoid(x):
    return 1.0 / (1.0 + jnp.exp(-x))


def _silu(x):
    return x * _sigmoid(x)


def _row_rms_scale(y):
    return lax.rsqrt(jnp.mean(y * y, axis=-1, keepdims=True) + EPS)


def _prenorm_kernel(x_ref, g_ref, o_ref):
    x = x_ref[...]
    o_ref[...] = (x * _row_rms_scale(x) * g_ref[...]).astype(o_ref.dtype)


def _prenorm(x2d, gain, tm=512):
    t, d = x2d.shape
    return pl.pallas_call(
        _prenorm_kernel,
        grid=(t // tm,),
        in_specs=[pl.BlockSpec((tm, d), lambda i: (i, 0)),
                  pl.BlockSpec((1, d), lambda i: (0, 0))],
        out_specs=pl.BlockSpec((tm, d), lambda i: (i, 0)),
        out_shape=jax.ShapeDtypeStruct((t, d), BF16),
        compiler_params=_cparams(("parallel",)),
        name="prenorm",
    )(x2d, gain.reshape(1, d))


def _proj_kernel(a_ref, w_ref, *rest, epilogue, n_extra):
    acc = jnp.dot(a_ref[...], w_ref[...], preferred_element_type=F32)
    epilogue(acc, rest[:n_extra], rest[n_extra:])


PROJ_TM = 1024
PROJ_TN = 1024


def _proj(a, w, extras, extra_specs, out_dtypes, epilogue, name, tm=PROJ_TM, tn=PROJ_TN):
    t, k = a.shape
    n = w.shape[1]
    kern = functools.partial(_proj_kernel, epilogue=epilogue, n_extra=len(extras))
    outs = pl.pallas_call(
        kern,
        grid=(n // tn, t // tm),
        in_specs=[pl.BlockSpec((tm, k), lambda j, i: (i, 0)),
                  pl.BlockSpec((k, tn), lambda j, i: (0, j))] + list(extra_specs),
        out_specs=[pl.BlockSpec((tm, tn), lambda j, i: (i, j)) for _ in out_dtypes],
        out_shape=[jax.ShapeDtypeStruct((t, n), dt) for dt in out_dtypes],
        compiler_params=_cparams(("parallel", "parallel")),
        name=name,
    )(a, w, *extras)
    return outs


def _head_spec():
    return pl.BlockSpec((1, HEAD_DIM), lambda j, i: (0, 0))


def _ep_cast(acc, extras, outs):
    outs[0][...] = acc.astype(outs[0].dtype)


def _ep_silu(acc, extras, outs):
    outs[0][...] = _silu(acc).astype(outs[0].dtype)


def _ep_headnorm(acc, extras, outs, *, scale):
    gain = extras[0][...] * scale
    for h in range(acc.shape[1] // HEAD_DIM):
        cols = slice(h * HEAD_DIM, (h + 1) * HEAD_DIM)
        y = acc[:, cols]
        outs[0][:, cols] = (y * _row_rms_scale(y) * gain).astype(outs[0].dtype)


def _ep_forget(acc, extras, outs, *, layer):
    logits = extras[0][...]
    e = jnp.exp(logits - jnp.max(logits, axis=0, keepdims=True))
    gam = e / jnp.sum(e, axis=0, keepdims=True)
    lb = jnp.sum(gam[:layer + 1], axis=0, keepdims=True) - gam[0:1]
    forget = lb + (1.0 - lb) * _sigmoid(acc)
    outs[0][...] = jnp.log(forget)
    outs[1][...] = (1.0 - forget).astype(outs[1].dtype)


def _fgate_kernel(a_ref, w_ref, b_ref, c_ref, carry_ref, *, tm):
    @pl.when(pl.program_id(1) == 0)
    def _():
        carry_ref[...] = jnp.zeros_like(carry_ref)

    f = jnp.dot(a_ref[0], w_ref[...], preferred_element_type=F32) + b_ref[...]
    x = jnp.minimum(f, 0.0) - jnp.log(1.0 + jnp.exp(-jnp.abs(f)))
    rows = lax.broadcasted_iota(jnp.int32, x.shape, 0)
    d = 1
    while d < tm:
        x = x + jnp.where(rows >= d, pltpu.roll(x, d, axis=0), 0.0)
        d *= 2
    x = x + carry_ref[...]
    c_ref[0] = x
    carry_ref[...] = x[tm - 1:tm, :]


def _fgate(xn3, w_f, b_f, tm=512):
    b, s, d = xn3.shape
    lanes = w_f.shape[1]
    return pl.pallas_call(
        functools.partial(_fgate_kernel, tm=tm),
        grid=(b, s // tm),
        in_specs=[pl.BlockSpec((1, tm, d), lambda bi, i: (bi, i, 0)),
                  pl.BlockSpec((d, lanes), lambda bi, i: (0, 0)),
                  pl.BlockSpec((1, lanes), lambda bi, i: (0, 0))],
        out_specs=pl.BlockSpec((1, tm, lanes), lambda bi, i: (bi, i, 0)),
        out_shape=jax.ShapeDtypeStruct((b, s, lanes), F32),
        scratch_shapes=[pltpu.VMEM((1, lanes), F32)],
        compiler_params=_cparams(("parallel", "arbitrary")),
        name="fox_fgate",
    )(xn3, w_f, b_f)


def _fox_attn_kernel(q_ref, k_ref, v_ref, ck_ref, z_ref, o_ref,
                     m_ref, l_ref, acc_ref, *, blk):
    i = pl.program_id(2)
    q = q_ref[0]
    m_ref[...] = jnp.full_like(m_ref, NEG)
    l_ref[...] = jnp.zeros_like(l_ref)
    acc_ref[...] = jnp.zeros_like(acc_ref)

    def update(j, masked):
        r0 = pl.multiple_of(j * blk, blk)
        kj = k_ref[0, pl.ds(r0, blk), :]
        vj = v_ref[0, pl.ds(r0, blk), :]
        s = lax.dot_general(q, kj, (((1,), (1,)), ((), ())),
                            preferred_element_type=F32)
        s = s - ck_ref[0, pl.ds(j, 1), :]
        if masked:
            rows = lax.broadcasted_iota(jnp.int32, s.shape, 0)
            cols = lax.broadcasted_iota(jnp.int32, s.shape, 1)
            s = jnp.where(cols <= rows, s, NEG)
        m_old = m_ref[...]
        m_new = jnp.maximum(m_old, jnp.max(s, axis=-1, keepdims=True))
        alpha = jnp.exp(m_old - m_new)
        p = jnp.exp(s - m_new)
        l_ref[...] = alpha * l_ref[...] + jnp.sum(p, axis=-1, keepdims=True)
        acc_ref[...] = alpha * acc_ref[...] + jnp.dot(
            p.astype(BF16), vj, preferred_element_type=F32)
        m_ref[...] = m_new

    def body(j, carry):
        update(j, masked=False)
        return carry

    lax.fori_loop(0, i, body, 0)
    update(i, masked=True)
    o = acc_ref[...] * (1.0 / l_ref[...])
    o_ref[0] = (o * z_ref[0].astype(F32)).astype(o_ref.dtype)


def _fox_attn(q3, k3, v3, ck, zs3, n_heads, blk=512):
    b, s, d = q3.shape
    nblk = s // blk
    qspec = pl.BlockSpec((1, blk, HEAD_DIM), lambda bi, h, i: (bi, i, h))
    kvspec = pl.BlockSpec((1, s, HEAD_DIM), lambda bi, h, i: (bi, 0, h))
    return pl.pallas_call(
        functools.partial(_fox_attn_kernel, blk=blk),
        grid=(b, n_heads, nblk),
        in_specs=[qspec, kvspec, kvspec,
                  pl.BlockSpec((1, nblk, blk), lambda bi, h, i: (bi * n_heads + h, 0, 0)),
                  qspec],
        out_specs=qspec,
        out_shape=jax.ShapeDtypeStruct((b, s, d), BF16),
        scratch_shapes=[pltpu.VMEM((blk, 1), F32), pltpu.VMEM((blk, 1), F32),
                        pltpu.VMEM((blk, HEAD_DIM), F32)],
        compiler_params=_cparams(("parallel", "parallel", "arbitrary")),
        name="fox_attn",
    )(q3, k3, v3, ck, zs3)


def _hgrn_kernel(q_ref, k_ref, g_ref, v_ref, z_ref, go_ref, o_ref, st_ref, *, tt, hb):
    c_len = HG_CHUNK

    @pl.when(pl.program_id(2) == 0)
    def _():
        st_ref[...] = jnp.zeros_like(st_ref)

    rows = lax.broadcasted_iota(jnp.int32, (c_len, HEAD_DIM), 0)
    ti = lax.broadcasted_iota(jnp.int32, (c_len, c_len), 0)
    si = lax.broadcasted_iota(jnp.int32, (c_len, c_len), 1)
    txs = ti ^ si
    go = go_ref[...]

    def chunk(ci, carry):
        r0 = pl.multiple_of(ci * c_len, c_len)
        for h in range(hb):
            cols = slice(h * HEAD_DIM, (h + 1) * HEAD_DIM)
            q = q_ref[0, pl.ds(r0, c_len), cols].astype(F32)
            k = k_ref[0, pl.ds(r0, c_len), cols].astype(F32)
            g = g_ref[0, pl.ds(r0, c_len), cols]
            v = v_ref[0, pl.ds(r0, c_len), cols]
            zs = z_ref[0, pl.ds(r0, c_len), cols].astype(F32)

            att = jnp.where(ti == si, jnp.sum(q * k, axis=-1, keepdims=True), 0.0)
            seg = g
            tot = g
            for n in HG_LEVELS:
                upper = (rows & n) != 0
                e = jnp.exp(jnp.where(upper, seg, tot - seg))
                a_n = lax.dot_general((q * e).astype(BF16), (k * e).astype(BF16),
                                      (((1,), (1,)), ((), ())),
                                      preferred_element_type=F32)
                pair = (ti > si) & (txs >= n) & (txs < 2 * n)
                att = jnp.where(pair, a_n, att)
                sib = jnp.where(upper, pltpu.roll(tot, n, axis=0),
                                pltpu.roll(tot, c_len - n, axis=0))
                seg = seg + jnp.where(upper, sib, 0.0)
                tot = tot + sib

            st = st_ref[h]
            inter = lax.dot_general((q * jnp.exp(seg)).astype(BF16), st.astype(BF16),
                                    (((1,), (1,)), ((), ())),
                                    preferred_element_type=F32)
            intra = jnp.dot(att.astype(BF16), v, preferred_element_type=F32)
            kv = lax.dot_general(v, (k * jnp.exp(tot - seg)).astype(BF16),
                                 (((0,), (0,)), ((), ())),
                                 preferred_element_type=F32)
            st_ref[h] = st * jnp.exp(tot[0:1, :]) + kv

            y = inter + intra
            y = y * _row_rms_scale(y) * go
            o_ref[0, pl.ds(r0, c_len), cols] = (y * zs).astype(o_ref.dtype)
        return carry

    lax.fori_loop(0, tt // c_len, chunk, 0)


def _hgrn(q3, k3, g3, v3, zs3, g_o, tt=256, hb=4):
    b, s, d = q3.shape
    n_groups = d // (HEAD_DIM * hb)
    spec = pl.BlockSpec((1, tt, HEAD_DIM * hb), lambda bi, hg, t: (bi, t, hg))
    return pl.pallas_call(
        functools.partial(_hgrn_kernel, tt=tt, hb=hb),
        grid=(b, n_groups, s // tt),
        in_specs=[spec, spec, spec, spec, spec,
                  pl.BlockSpec((1, HEAD_DIM), lambda bi, hg, t: (0, 0))],
        out_specs=spec,
        out_shape=jax.ShapeDtypeStruct((b, s, d), BF16),
        scratch_shapes=[pltpu.VMEM((hb, HEAD_DIM, HEAD_DIM), F32)],
        compiler_params=_cparams(("parallel", "parallel", "arbitrary")),
        name="hgrn2_scan",
    )(q3, k3, g3, v3, zs3, g_o.reshape(1, HEAD_DIM))


def _out_kernel(og_ref, h_ref, p_ref, wo_ref, wpe_ref, wpg_ref, post_ref, *rest, emit_xn):
    y = jnp.dot(og_ref[...], wo_ref[...], preferred_element_type=F32)
    h1 = h_ref[...] + y * _row_rms_scale(y) * post_ref[...]
    pe = jnp.dot(p_ref[...].astype(BF16), wpe_ref[...], preferred_element_type=F32)
    gate = _sigmoid(jnp.dot(h1.astype(BF16), wpg_ref[...], preferred_element_type=F32))
    h2 = h1 + pe * gate
    if emit_xn:
        pre_ref, h_out_ref, xn_ref = rest
        xn_ref[...] = (h2 * _row_rms_scale(h2) * pre_ref[...]).astype(xn_ref.dtype)
    else:
        (h_out_ref,) = rest
    h_out_ref[...] = h2


def _out_block(og, h, p_all, layer, w_out, w_pe, w_pg, post, pre_next, tm=512):
    t, d = h.shape
    pdim = p_all.shape[-1]
    emit_xn = pre_next is not None
    row = lambda i: (i, 0)
    whole = lambda i: (0, 0)
    const = dict(pipeline_mode=pl.Buffered(1))
    in_specs = [pl.BlockSpec((tm, d), row),
                pl.BlockSpec((tm, d), row),
                pl.BlockSpec((None, tm, pdim), lambda i: (layer, i, 0)),
                pl.BlockSpec((d, d), whole, **const),
                pl.BlockSpec((pdim, d), whole, **const),
                pl.BlockSpec((d, d), whole, **const),
                pl.BlockSpec((1, d), whole)]
    args = [og, h, p_all, w_out, w_pe, w_pg, post.reshape(1, d)]
    out_specs = [pl.BlockSpec((tm, d), row)]
    out_shape = [jax.ShapeDtypeStruct((t, d), F32)]
    if emit_xn:
        in_specs.append(pl.BlockSpec((1, d), whole))
        args.append(pre_next.reshape(1, d))
        out_specs.append(pl.BlockSpec((tm, d), row))
        out_shape.append(jax.ShapeDtypeStruct((t, d), BF16))
    outs = pl.pallas_call(
        functools.partial(_out_kernel, emit_xn=emit_xn),
        grid=(t // tm,),
        in_specs=in_specs,
        out_specs=out_specs,
        out_shape=out_shape,
        compiler_params=_cparams(("parallel",)),
        name=f"out_block_{layer}",
    )(*args)
    return outs if emit_xn else (outs[0], None)


def _fox_layer(xn, b, s, w_in, b_f, g_q, g_k):
    t, d = xn.shape
    n_heads = d // HEAD_DIM
    w = w_in.astype(BF16)
    scale = 1.0 / math.sqrt(HEAD_DIM)
    gq = g_q.reshape(1, HEAD_DIM)
    gk = g_k.reshape(1, HEAD_DIM)
    (q,) = _proj(xn, w[:, 0:d], [gq], [_head_spec()], [BF16],
                 functools.partial(_ep_headnorm, scale=scale), "fox_proj_q")
    (k,) = _proj(xn, w[:, d:2 * d], [gk], [_head_spec()], [BF16],
                 functools.partial(_ep_headnorm, scale=1.0), "fox_proj_k")
    (v,) = _proj(xn, w[:, 2 * d:3 * d], [], [], [BF16], _ep_cast, "fox_proj_v")
    (zs,) = _proj(xn, w[:, 3 * d:4 * d], [], [], [BF16], _ep_silu, "fox_proj_z")

    lanes = HEAD_DIM
    w_f = jnp.zeros((d, lanes), BF16).at[:, :n_heads].set(w[:, 4 * d:])
    b_fp = jnp.zeros((1, lanes), F32).at[0, :n_heads].set(b_f)
    c = _fgate(xn.reshape(b, s, d), w_f, b_fp)

    blk = 512
    ck = jnp.transpose(c[:, :, :n_heads], (0, 2, 1)).reshape(b * n_heads, s // blk, blk)
    r3 = lambda a: a.reshape(b, s, d)
    og = _fox_attn(r3(q), r3(k), r3(v), ck, r3(zs), n_heads, blk=blk)
    return og.reshape(t, d)


def _hgrn_layer(xn, b, s, layer, w_in, lb_logits, g_o):
    t, d = xn.shape
    w = w_in.astype(BF16)
    (q,) = _proj(xn, w[:, 0:d], [], [], [BF16], _ep_silu, "hg_proj_q")
    depth = lb_logits.shape[0]
    g, k = _proj(xn, w[:, d:2 * d], [lb_logits],
                 [pl.BlockSpec((depth, PROJ_TN), lambda j, i: (0, j))], [F32, BF16],
                 functools.partial(_ep_forget, layer=layer), "hg_proj_f")
    (v,) = _proj(xn, w[:, 2 * d:3 * d], [], [], [BF16], _ep_cast, "hg_proj_i")
    (zs,) = _proj(xn, w[:, 3 * d:4 * d], [], [], [BF16], _ep_silu, "hg_proj_z")
    r3 = lambda a: a.reshape(b, s, d)
    og = _hgrn(r3(q), r3(k), r3(g), r3(v), r3(zs), g_o)
    return og.reshape(t, d)


def kernel(x, p, w_in_fox, b_f_fox, g_q_fox, g_k_fox, w_out_fox, w_in_hg, lb_logits,
           g_o_hg, w_out_hg, pre_norm, post_norm, w_pe, w_pg):
    b, s, d = x.shape
    depth = p.shape[0]
    t = b * s
    h = x.reshape(t, d)
    p_all = p.reshape(depth, t, p.shape[-1])
    xn = _prenorm(h, pre_norm[0])
    for i in range(depth):
        j = i // 2
        if i % 2 == 0:
            og = _fox_layer(xn, b, s, w_in_fox[j], b_f_fox[j], g_q_fox[j], g_k_fox[j])
            w_out = w_out_fox[j]
        else:
            og = _hgrn_layer(xn, b, s, i, w_in_hg[j], lb_logits, g_o_hg[j])
            w_out = w_out_hg[j]
        pre_next = pre_norm[i + 1] if i + 1 < depth else None
        h, xn = _out_block(og, h, p_all, i, w_out.astype(BF16), w_pe[i].astype(BF16),
                           w_pg[i].astype(BF16), post_norm[i], pre_next)
    return h.reshape(b, s, d)
```

```python
import functools
import math

import jax
import jax.numpy as jnp
from jax import lax
from jax.experimental import pallas as pl
from jax.experimental.pallas import tpu as pltpu

HEAD_DIM = 128
EPS = 1e-6
NEG = -1e30
LOG2E = math.log2(math.e)
ONES_ROWS = 16
HG_CHUNK = 64
HG_LEVELS = (1, 2, 4, 8, 16, 32)

F32 = jnp.float32
BF16 = jnp.bfloat16

VMEM_LIMIT_BYTES = 56 * 1024 * 1024


def _cparams(semantics):
    return pltpu.CompilerParams(dimension_semantics=semantics,
                                vmem_limit_bytes=VMEM_LIMIT_BYTES)


def _sigmoid(x):
    return 1.0 / (1.0 + jnp.exp(-x))


def _silu(x):
    return x * _sigmoid(x)


def _row_rms_scale(y):
    return lax.rsqrt(jnp.mean(y * y, axis=-1, keepdims=True) + EPS)


def _prenorm_kernel(x_ref, g_ref, o_ref):
    x = x_ref[...]
    o_ref[...] = (x * _row_rms_scale(x) * g_ref[...]).astype(o_ref.dtype)


def _prenorm(x2d, gain, tm=512):
    t, d = x2d.shape
    return pl.pallas_call(
        _prenorm_kernel,
        grid=(t // tm,),
        in_specs=[pl.BlockSpec((tm, d), lambda i: (i, 0)),
                  pl.BlockSpec((1, d), lambda i: (0, 0))],
        out_specs=pl.BlockSpec((tm, d), lambda i: (i, 0)),
        out_shape=jax.ShapeDtypeStruct((t, d), BF16),
        compiler_params=_cparams(("parallel",)),
        name="prenorm",
    )(x2d, gain.reshape(1, d))


def _proj_kernel(a_ref, w_ref, *rest, epilogue, n_extra):
    acc = jnp.dot(a_ref[...], w_ref[...], preferred_element_type=F32)
    epilogue(acc, rest[:n_extra], rest[n_extra:])


PROJ_TM = 1024
PROJ_TN = 1024


def _proj(a, w, extras, extra_specs, out_dtypes, epilogue, name, tm=PROJ_TM, tn=PROJ_TN):
    t, k = a.shape
    n = w.shape[1]
    kern = functools.partial(_proj_kernel, epilogue=epilogue, n_extra=len(extras))
    outs = pl.pallas_call(
        kern,
        grid=(n // tn, t // tm),
        in_specs=[pl.BlockSpec((tm, k), lambda j, i: (i, 0)),
                  pl.BlockSpec((k, tn), lambda j, i: (0, j))] + list(extra_specs),
        out_specs=[pl.BlockSpec((tm, tn), lambda j, i: (i, j)) for _ in out_dtypes],
        out_shape=[jax.ShapeDtypeStruct((t, n), dt) for dt in out_dtypes],
        compiler_params=_cparams(("parallel", "parallel")),
        name=name,
    )(a, w, *extras)
    return outs


def _head_spec():
    return pl.BlockSpec((1, HEAD_DIM), lambda j, i: (0, 0))


def _ep_cast(acc, extras, outs):
    outs[0][...] = acc.astype(outs[0].dtype)


def _ep_silu(acc, extras, outs):
    outs[0][...] = _silu(acc).astype(outs[0].dtype)


def _ep_headnorm(acc, extras, outs, *, scale):
    gain = extras[0][...] * scale
    for h in range(acc.shape[1] // HEAD_DIM):
        cols = slice(h * HEAD_DIM, (h + 1) * HEAD_DIM)
        y = acc[:, cols]
        outs[0][:, cols] = (y * _row_rms_scale(y) * gain).astype(outs[0].dtype)


def _ep_forget(acc, extras, outs, *, layer):
    logits = extras[0][...]
    e = jnp.exp(logits - jnp.max(logits, axis=0, keepdims=True))
    gam = e / jnp.sum(e, axis=0, keepdims=True)
    lb = jnp.sum(gam[:layer + 1], axis=0, keepdims=True) - gam[0:1]
    forget = lb + (1.0 - lb) * _sigmoid(acc)
    outs[0][...] = jnp.log(forget)
    outs[1][...] = (1.0 - forget).astype(outs[1].dtype)


def _fgate_kernel(a_ref, w_ref, b_ref, c_ref, carry_ref, *, tm, n_heads):
    @pl.when(pl.program_id(1) == 0)
    def _():
        carry_ref[...] = jnp.zeros_like(carry_ref)

    f = jnp.dot(a_ref[0], w_ref[...], preferred_element_type=F32) + b_ref[...]
    x = jnp.minimum(f, 0.0) - jnp.log(1.0 + jnp.exp(-jnp.abs(f)))
    rows = lax.broadcasted_iota(jnp.int32, x.shape, 0)
    d = 1
    while d < tm:
        x = x + jnp.where(rows >= d, pltpu.roll(x, d, axis=0), 0.0)
        d *= 2
    x = x + carry_ref[...]
    carry_ref[...] = x[tm - 1:tm, :]
    x2 = x * LOG2E
    for h in range(n_heads):
        c_ref[0, h] = jnp.broadcast_to(x2[:, h:h + 1], x.shape)


def _fgate(xn3, w_f, b_f, n_heads, tm=512):
    b, s, d = xn3.shape
    lanes = w_f.shape[1]
    return pl.pallas_call(
        functools.partial(_fgate_kernel, tm=tm, n_heads=n_heads),
        grid=(b, s // tm),
        in_specs=[pl.BlockSpec((1, tm, d), lambda bi, i: (bi, i, 0)),
                  pl.BlockSpec((d, lanes), lambda bi, i: (0, 0)),
                  pl.BlockSpec((1, lanes), lambda bi, i: (0, 0))],
        out_specs=pl.BlockSpec((1, n_heads, tm, lanes), lambda bi, i: (bi, 0, i, 0)),
        out_shape=jax.ShapeDtypeStruct((b, n_heads, s, lanes), F32),
        scratch_shapes=[pltpu.VMEM((1, lanes), F32)],
        compiler_params=_cparams(("parallel", "arbitrary")),
        name="fox_fgate",
    )(xn3, w_f, b_f)


def _fox_attn_kernel(q_ref, k_ref, v_ref, ck_ref, z_ref, o_ref, acc_ref, *, qb, kb):
    i = pl.program_id(2)
    q = q_ref[0]
    per_tile = qb // kb
    acc_ref[...] = jnp.zeros_like(acc_ref)
    ones = jnp.ones((ONES_ROWS, kb), BF16)

    def update(j, m_old, masked):
        r0 = pl.multiple_of(j * kb, kb)
        kj = k_ref[0, pl.ds(r0, kb), :]
        vt = jnp.concatenate([v_ref[0, pl.ds(r0, kb), :].T, ones], axis=0)
        ckj = ck_ref[0, 0, pl.ds(r0, kb), :]
        s = lax.dot_general(kj, q, (((1,), (1,)), ((), ())),
                            preferred_element_type=F32)
        s = s - jnp.concatenate([ckj] * (qb // ckj.shape[1]), axis=1)
        if masked:
            kpos = r0 + lax.broadcasted_iota(jnp.int32, s.shape, 0)
            qpos = i * qb + lax.broadcasted_iota(jnp.int32, s.shape, 1)
            s = jnp.where(kpos <= qpos, s, NEG)
        m_new = jnp.maximum(m_old, jnp.max(s, axis=0, keepdims=True))
        alpha = jnp.exp2(m_old - m_new)
        p = jnp.exp2(s - m_new).astype(BF16)
        acc_ref[...] = alpha * acc_ref[...] + jnp.dot(
            vt, p, preferred_element_type=F32)
        return m_new

    def body(jt, m):
        for u in range(per_tile):
            m = update(jt * per_tile + u, m, masked=False)
        return m

    m = lax.fori_loop(0, i, body, jnp.full((1, qb), NEG, F32))
    for u in range(per_tile):
        m = update(i * per_tile + u, m, masked=True)
    l = acc_ref[HEAD_DIM:HEAD_DIM + 1, :]
    o = (acc_ref[0:HEAD_DIM, :] * (1.0 / l)).T
    o_ref[0] = (o * z_ref[0].astype(F32)).astype(o_ref.dtype)


def _fox_attn(q3, k3, v3, ck, zs3, n_heads, qb=1024, kb=512):
    b, s, d = q3.shape
    lanes = ck.shape[-1]
    qspec = pl.BlockSpec((1, qb, HEAD_DIM), lambda bi, h, i: (bi, i, h))
    kvspec = pl.BlockSpec((1, s, HEAD_DIM), lambda bi, h, i: (bi, 0, h))
    return pl.pallas_call(
        functools.partial(_fox_attn_kernel, qb=qb, kb=kb),
        grid=(b, n_heads, s // qb),
        in_specs=[qspec, kvspec, kvspec,
                  pl.BlockSpec((1, 1, s, lanes), lambda bi, h, i: (bi, h, 0, 0)),
                  qspec],
        out_specs=qspec,
        out_shape=jax.ShapeDtypeStruct((b, s, d), BF16),
        scratch_shapes=[pltpu.VMEM((HEAD_DIM + ONES_ROWS, qb), F32)],
        compiler_params=_cparams(("parallel", "parallel", "arbitrary")),
        name="fox_attn",
    )(q3, k3, v3, ck, zs3)


def _hgrn_kernel(q_ref, k_ref, g_ref, v_ref, z_ref, go_ref, o_ref, st_ref, *, tt, hb):
    c_len = HG_CHUNK

    @pl.when(pl.program_id(2) == 0)
    def _():
        st_ref[...] = jnp.zeros_like(st_ref)

    rows = lax.broadcasted_iota(jnp.int32, (c_len, HEAD_DIM), 0)
    ti = lax.broadcasted_iota(jnp.int32, (c_len, c_len), 0)
    si = lax.broadcasted_iota(jnp.int32, (c_len, c_len), 1)
    txs = ti ^ si
    go = go_ref[...]

    def chunk(ci, carry):
        r0 = pl.multiple_of(ci * c_len, c_len)
        for h in range(hb):
            cols = slice(h * HEAD_DIM, (h + 1) * HEAD_DIM)
            q = q_ref[0, pl.ds(r0, c_len), cols].astype(F32)
            k = k_ref[0, pl.ds(r0, c_len), cols].astype(F32)
            g = g_ref[0, pl.ds(r0, c_len), cols]
            v = v_ref[0, pl.ds(r0, c_len), cols]
            zs = z_ref[0, pl.ds(r0, c_len), cols].astype(F32)

            att = jnp.where(ti == si, jnp.sum(q * k, axis=-1, keepdims=True), 0.0)
            seg = g
            tot = g
            for n in HG_LEVELS:
                upper = (rows & n) != 0
                e = jnp.exp(jnp.where(upper, seg, tot - seg))
                a_n = lax.dot_general((q * e).astype(BF16), (k * e).astype(BF16),
                                      (((1,), (1,)), ((), ())),
                                      preferred_element_type=F32)
                pair = (ti > si) & (txs >= n) & (txs < 2 * n)
                att = jnp.where(pair, a_n, att)
                sib = jnp.where(upper, pltpu.roll(tot, n, axis=0),
                                pltpu.roll(tot, c_len - n, axis=0))
                seg = seg + jnp.where(upper, sib, 0.0)
                tot = tot + sib

            st = st_ref[h]
            inter = lax.dot_general((q * jnp.exp(seg)).astype(BF16), st.astype(BF16),
                                    (((1,), (1,)), ((), ())),
                                    preferred_element_type=F32)
            intra = jnp.dot(att.astype(BF16), v, preferred_element_type=F32)
            kv = lax.dot_general(v, (k * jnp.exp(tot - seg)).astype(BF16),
                                 (((0,), (0,)), ((), ())),
                                 preferred_element_type=F32)
            st_ref[h] = st * jnp.exp(tot[0:1, :]) + kv

            y = inter + intra
            y = y * _row_rms_scale(y) * go
            o_ref[0, pl.ds(r0, c_len), cols] = (y * zs).astype(o_ref.dtype)
        return carry

    lax.fori_loop(0, tt // c_len, chunk, 0)


def _hgrn(q3, k3, g3, v3, zs3, g_o, tt=256, hb=4):
    b, s, d = q3.shape
    n_groups = d // (HEAD_DIM * hb)
    spec = pl.BlockSpec((1, tt, HEAD_DIM * hb), lambda bi, hg, t: (bi, t, hg))
    return pl.pallas_call(
        functools.partial(_hgrn_kernel, tt=tt, hb=hb),
        grid=(b, n_groups, s // tt),
        in_specs=[spec, spec, spec, spec, spec,
                  pl.BlockSpec((1, HEAD_DIM), lambda bi, hg, t: (0, 0))],
        out_specs=spec,
        out_shape=jax.ShapeDtypeStruct((b, s, d), BF16),
        scratch_shapes=[pltpu.VMEM((hb, HEAD_DIM, HEAD_DIM), F32)],
        compiler_params=_cparams(("parallel", "parallel", "arbitrary")),
        name="hgrn2_scan",
    )(q3, k3, g3, v3, zs3, g_o.reshape(1, HEAD_DIM))


def _out_kernel(og_ref, h_ref, p_ref, wo_ref, wpe_ref, wpg_ref, post_ref, *rest, emit_xn):
    y = jnp.dot(og_ref[...], wo_ref[...], preferred_element_type=F32)
    h1 = h_ref[...] + y * _row_rms_scale(y) * post_ref[...]
    pe = jnp.dot(p_ref[...].astype(BF16), wpe_ref[...], preferred_element_type=F32)
    gate = _sigmoid(jnp.dot(h1.astype(BF16), wpg_ref[...], preferred_element_type=F32))
    h2 = h1 + pe * gate
    if emit_xn:
        pre_ref, h_out_ref, xn_ref = rest
        xn_ref[...] = (h2 * _row_rms_scale(h2) * pre_ref[...]).astype(xn_ref.dtype)
    else:
        (h_out_ref,) = rest
    h_out_ref[...] = h2


def _out_block(og, h, p_all, layer, w_out, w_pe, w_pg, post, pre_next, tm=512):
    t, d = h.shape
    pdim = p_all.shape[-1]
    emit_xn = pre_next is not None
    row = lambda i: (i, 0)
    whole = lambda i: (0, 0)
    const = dict(pipeline_mode=pl.Buffered(1))
    in_specs = [pl.BlockSpec((tm, d), row),
                pl.BlockSpec((tm, d), row),
                pl.BlockSpec((None, tm, pdim), lambda i: (layer, i, 0)),
                pl.BlockSpec((d, d), whole, **const),
                pl.BlockSpec((pdim, d), whole, **const),
                pl.BlockSpec((d, d), whole, **const),
                pl.BlockSpec((1, d), whole)]
    args = [og, h, p_all, w_out, w_pe, w_pg, post.reshape(1, d)]
    out_specs = [pl.BlockSpec((tm, d), row)]
    out_shape = [jax.ShapeDtypeStruct((t, d), F32)]
    if emit_xn:
        in_specs.append(pl.BlockSpec((1, d), whole))
        args.append(pre_next.reshape(1, d))
        out_specs.append(pl.BlockSpec((tm, d), row))
        out_shape.append(jax.ShapeDtypeStruct((t, d), BF16))
    outs = pl.pallas_call(
        functools.partial(_out_kernel, emit_xn=emit_xn),
        grid=(t // tm,),
        in_specs=in_specs,
        out_specs=out_specs,
        out_shape=out_shape,
        compiler_params=_cparams(("parallel",)),
        name=f"out_block_{layer}",
    )(*args)
    return outs if emit_xn else (outs[0], None)


def _fox_layer(xn, b, s, w_in, b_f, g_q, g_k):
    t, d = xn.shape
    n_heads = d // HEAD_DIM
    w = w_in.astype(BF16)
    scale = LOG2E / math.sqrt(HEAD_DIM)
    gq = g_q.reshape(1, HEAD_DIM)
    gk = g_k.reshape(1, HEAD_DIM)
    (q,) = _proj(xn, w[:, 0:d], [gq], [_head_spec()], [BF16],
                 functools.partial(_ep_headnorm, scale=scale), "fox_proj_q")
    (k,) = _proj(xn, w[:, d:2 * d], [gk], [_head_spec()], [BF16],
                 functools.partial(_ep_headnorm, scale=1.0), "fox_proj_k")
    (v,) = _proj(xn, w[:, 2 * d:3 * d], [], [], [BF16], _ep_cast, "fox_proj_v")
    (zs,) = _proj(xn, w[:, 3 * d:4 * d], [], [], [BF16], _ep_silu, "fox_proj_z")

    lanes = HEAD_DIM
    w_f = jnp.zeros((d, lanes), BF16).at[:, :n_heads].set(w[:, 4 * d:])
    b_fp = jnp.zeros((1, lanes), F32).at[0, :n_heads].set(b_f)
    ck = _fgate(xn.reshape(b, s, d), w_f, b_fp, n_heads)

    r3 = lambda a: a.reshape(b, s, d)
    og = _fox_attn(r3(q), r3(k), r3(v), ck, r3(zs), n_heads)
    return og.reshape(t, d)


def _hgrn_layer(xn, b, s, layer, w_in, lb_logits, g_o):
    t, d = xn.shape
    w = w_in.astype(BF16)
    (q,) = _proj(xn, w[:, 0:d], [], [], [BF16], _ep_silu, "hg_proj_q")
    depth = lb_logits.shape[0]
    g, k = _proj(xn, w[:, d:2 * d], [lb_logits],
                 [pl.BlockSpec((depth, PROJ_TN), lambda j, i: (0, j))], [F32, BF16],
                 functools.partial(_ep_forget, layer=layer), "hg_proj_f")
    (v,) = _proj(xn, w[:, 2 * d:3 * d], [], [], [BF16], _ep_cast, "hg_proj_i")
    (zs,) = _proj(xn, w[:, 3 * d:4 * d], [], [], [BF16], _ep_silu, "hg_proj_z")
    r3 = lambda a: a.reshape(b, s, d)
    og = _hgrn(r3(q), r3(k), r3(g), r3(v), r3(zs), g_o)
    return og.reshape(t, d)


def kernel(x, p, w_in_fox, b_f_fox, g_q_fox, g_k_fox, w_out_fox, w_in_hg, lb_logits,
           g_o_hg, w_out_hg, pre_norm, post_norm, w_pe, w_pg):
    b, s, d = x.shape
    depth = p.shape[0]
    t = b * s
    h = x.reshape(t, d)
    p_all = p.reshape(depth, t, p.shape[-1])
    xn = _prenorm(h, pre_norm[0])
    for i in range(depth):
        j = i // 2
        if i % 2 == 0:
            og = _fox_layer(xn, b, s, w_in_fox[j], b_f_fox[j], g_q_fox[j], g_k_fox[j])
            w_out = w_out_fox[j]
        else:
            og = _hgrn_layer(xn, b, s, i, w_in_hg[j], lb_logits, g_o_hg[j])
            w_out = w_out_hg[j]
        pre_next = pre_norm[i + 1] if i + 1 < depth else None
        h, xn = _out_block(og, h, p_all, i, w_out.astype(BF16), w_pe[i].astype(BF16),
                           w_pg[i].astype(BF16), post_norm[i], pre_next)
    return h.reshape(b, s, d)
```

```python
import functools
import math

import jax
import jax.numpy as jnp
from jax import lax
from jax.experimental import pallas as pl
from jax.experimental.pallas import tpu as pltpu

HEAD_DIM = 128
EPS = 1e-6
NEG = -1e30
LOG2E = math.log2(math.e)
ONES_ROWS = 16
HG_CHUNK = 64
HG_LEVELS = (1, 2, 4, 8, 16, 32)
SUBLANES = 8

F32 = jnp.float32
BF16 = jnp.bfloat16

VMEM_LIMIT_BYTES = 56 * 1024 * 1024


def _cparams(semantics):
    return pltpu.CompilerParams(dimension_semantics=semantics,
                                vmem_limit_bytes=VMEM_LIMIT_BYTES)


def _sigmoid(x):
    return 1.0 / (1.0 + jnp.exp(-x))


def _silu(x):
    return x * _sigmoid(x)


def _row_rms_scale(y):
    return lax.rsqrt(jnp.mean(y * y, axis=-1, keepdims=True) + EPS)


def _prenorm_kernel(x_ref, g_ref, o_ref):
    x = x_ref[...]
    o_ref[...] = (x * _row_rms_scale(x) * g_ref[...]).astype(o_ref.dtype)


def _prenorm(x2d, gain, tm=512):
    t, d = x2d.shape
    return pl.pallas_call(
        _prenorm_kernel,
        grid=(t // tm,),
        in_specs=[pl.BlockSpec((tm, d), lambda i: (i, 0)),
                  pl.BlockSpec((1, d), lambda i: (0, 0))],
        out_specs=pl.BlockSpec((tm, d), lambda i: (i, 0)),
        out_shape=jax.ShapeDtypeStruct((t, d), BF16),
        compiler_params=_cparams(("parallel",)),
        name="prenorm",
    )(x2d, gain.reshape(1, d))


def _proj_kernel(a_ref, w_ref, *rest, epilogue, n_extra):
    acc = jnp.dot(a_ref[...], w_ref[...], preferred_element_type=F32)
    epilogue(acc, rest[:n_extra], rest[n_extra:])


PROJ_TM = 1024
PROJ_TN = 1024


def _proj(a, w, extras, extra_specs, out_dtypes, epilogue, name, tm=PROJ_TM, tn=PROJ_TN):
    t, k = a.shape
    n = w.shape[1]
    kern = functools.partial(_proj_kernel, epilogue=epilogue, n_extra=len(extras))
    outs = pl.pallas_call(
        kern,
        grid=(n // tn, t // tm),
        in_specs=[pl.BlockSpec((tm, k), lambda j, i: (i, 0)),
                  pl.BlockSpec((k, tn), lambda j, i: (0, j))] + list(extra_specs),
        out_specs=[pl.BlockSpec((tm, tn), lambda j, i: (i, j)) for _ in out_dtypes],
        out_shape=[jax.ShapeDtypeStruct((t, n), dt) for dt in out_dtypes],
        compiler_params=_cparams(("parallel", "parallel")),
        name=name,
    )(a, w, *extras)
    return outs


def _head_spec():
    return pl.BlockSpec((1, HEAD_DIM), lambda j, i: (0, 0))


def _ep_cast(acc, extras, outs):
    outs[0][...] = acc.astype(outs[0].dtype)


def _ep_silu(acc, extras, outs):
    outs[0][...] = _silu(acc).astype(outs[0].dtype)


def _ep_headnorm(acc, extras, outs, *, scale):
    gain = extras[0][...] * scale
    for h in range(acc.shape[1] // HEAD_DIM):
        cols = slice(h * HEAD_DIM, (h + 1) * HEAD_DIM)
        y = acc[:, cols]
        outs[0][:, cols] = (y * _row_rms_scale(y) * gain).astype(outs[0].dtype)


def _ep_forget(acc, extras, outs, *, layer):
    logits = extras[0][...]
    e = jnp.exp(logits - jnp.max(logits, axis=0, keepdims=True))
    gam = e / jnp.sum(e, axis=0, keepdims=True)
    lb = jnp.sum(gam[:layer + 1], axis=0, keepdims=True) - gam[0:1]
    forget = lb + (1.0 - lb) * _sigmoid(acc)
    outs[0][...] = jnp.log(forget)
    outs[1][...] = (1.0 - forget).astype(outs[1].dtype)


def _fgate_kernel(a_ref, w_ref, b_ref, c_ref, carry_ref, *, tm, n_heads):
    @pl.when(pl.program_id(1) == 0)
    def _():
        carry_ref[...] = jnp.zeros_like(carry_ref)

    f = jnp.dot(a_ref[0], w_ref[...], preferred_element_type=F32) + b_ref[...]
    x = jnp.minimum(f, 0.0) - jnp.log(1.0 + jnp.exp(-jnp.abs(f)))
    rows = lax.broadcasted_iota(jnp.int32, x.shape, 0)
    d = 1
    while d < tm:
        x = x + jnp.where(rows >= d, pltpu.roll(x, d, axis=0), 0.0)
        d *= 2
    x = x + carry_ref[...]
    carry_ref[...] = x[tm - 1:tm, :]
    x2 = x * LOG2E
    for h in range(n_heads):
        c_ref[0, h] = jnp.broadcast_to(x2[:, h:h + 1], x.shape)


def _fgate(xn3, w_f, b_f, n_heads, tm=512):
    b, s, d = xn3.shape
    lanes = w_f.shape[1]
    return pl.pallas_call(
        functools.partial(_fgate_kernel, tm=tm, n_heads=n_heads),
        grid=(b, s // tm),
        in_specs=[pl.BlockSpec((1, tm, d), lambda bi, i: (bi, i, 0)),
                  pl.BlockSpec((d, lanes), lambda bi, i: (0, 0)),
                  pl.BlockSpec((1, lanes), lambda bi, i: (0, 0))],
        out_specs=pl.BlockSpec((1, n_heads, tm, lanes), lambda bi, i: (bi, 0, i, 0)),
        out_shape=jax.ShapeDtypeStruct((b, n_heads, s, lanes), F32),
        scratch_shapes=[pltpu.VMEM((1, lanes), F32)],
        compiler_params=_cparams(("parallel", "arbitrary")),
        name="fox_fgate",
    )(xn3, w_f, b_f)


def _fox_attn_kernel(q_ref, k_ref, v_ref, ck_ref, z_ref, o_ref, acc_ref, s0_ref, s1_ref,
                     *, qb, kb):
    i = pl.program_id(2)
    q = q_ref[0]
    per_tile = qb // kb
    acc_ref[...] = jnp.zeros_like(acc_ref)
    ones = jnp.ones((ONES_ROWS, kb), BF16)

    def scores(t, s_ref):
        for u in range(per_tile):
            r0 = pl.multiple_of(t * qb + u * kb, kb)
            kj = k_ref[0, pl.ds(r0, kb), :]
            ckj = ck_ref[0, 0, pl.ds(r0, kb), :]
            s = lax.dot_general(kj, q, (((1,), (1,)), ((), ())),
                                preferred_element_type=F32)
            s_ref[u * kb:(u + 1) * kb, :] = s - jnp.concatenate(
                [ckj] * (qb // ckj.shape[1]), axis=1)

    def absorb(t, s_ref, m_old, masked):
        for u in range(per_tile):
            r0 = pl.multiple_of(t * qb + u * kb, kb)
            vt = jnp.concatenate([v_ref[0, pl.ds(r0, kb), :].T, ones], axis=0)
            s = s_ref[u * kb:(u + 1) * kb, :]
            if masked:
                kpos = r0 + lax.broadcasted_iota(jnp.int32, s.shape, 0)
                qpos = i * qb + lax.broadcasted_iota(jnp.int32, s.shape, 1)
                s = jnp.where(kpos <= qpos, s, NEG)
            m_new = jnp.maximum(m_old, jnp.max(s, axis=0, keepdims=True))
            alpha = jnp.exp2(m_old - m_new)
            p = jnp.exp2(s - m_new).astype(BF16)
            acc_ref[...] = alpha * acc_ref[...] + jnp.dot(
                vt, p, preferred_element_type=F32)
            m_old = m_new
        return m_old

    def body(a, m):
        t = 2 * a
        scores(t + 1, s1_ref)
        m = absorb(t, s0_ref, m, masked=False)
        scores(t + 2, s0_ref)
        return absorb(t + 1, s1_ref, m, masked=False)

    scores(0, s0_ref)
    m = lax.fori_loop(0, i // 2, body, jnp.full((1, qb), NEG, F32))
    odd = lax.rem(i, 2) == 1

    @pl.when(jnp.logical_not(odd))
    def _():
        absorb(i, s0_ref, m, masked=True)

    @pl.when(odd)
    def _():
        scores(i, s1_ref)
        m1 = absorb(i - 1, s0_ref, m, masked=False)
        absorb(i, s1_ref, m1, masked=True)

    l = acc_ref[HEAD_DIM:HEAD_DIM + 1, :]
    o = (acc_ref[0:HEAD_DIM, :] * (1.0 / l)).T
    o_ref[0] = (o * z_ref[0].astype(F32)).astype(o_ref.dtype)


def _fox_attn(q3, k3, v3, ck, zs3, n_heads, qb=1024, kb=512):
    b, s, d = q3.shape
    lanes = ck.shape[-1]
    qspec = pl.BlockSpec((1, qb, HEAD_DIM), lambda bi, h, i: (bi, i, h))
    kvspec = pl.BlockSpec((1, s, HEAD_DIM), lambda bi, h, i: (bi, 0, h))
    return pl.pallas_call(
        functools.partial(_fox_attn_kernel, qb=qb, kb=kb),
        grid=(b, n_heads, s // qb),
        in_specs=[qspec, kvspec, kvspec,
                  pl.BlockSpec((1, 1, s, lanes), lambda bi, h, i: (bi, h, 0, 0)),
                  qspec],
        out_specs=qspec,
        out_shape=jax.ShapeDtypeStruct((b, s, d), BF16),
        scratch_shapes=[pltpu.VMEM((HEAD_DIM + ONES_ROWS, qb), F32),
                        pltpu.VMEM((qb, qb), F32), pltpu.VMEM((qb, qb), F32)],
        compiler_params=_cparams(("parallel", "parallel", "arbitrary")),
        name="fox_attn",
    )(q3, k3, v3, ck, zs3)


def _hgrn_kernel(q_ref, k_ref, g_ref, v_ref, z_ref, go_ref, o_ref, st_ref, *, tt, hb):
    c_len = HG_CHUNK

    @pl.when(pl.program_id(2) == 0)
    def _():
        st_ref[...] = jnp.zeros_like(st_ref)

    n_tiles = c_len // SUBLANES
    sub = lax.broadcasted_iota(jnp.int32, (SUBLANES, HEAD_DIM), 0)
    ti = lax.broadcasted_iota(jnp.int32, (c_len, c_len), 0)
    si = lax.broadcasted_iota(jnp.int32, (c_len, c_len), 1)
    txs = ti ^ si
    eye = ti == si
    pair = {n: (ti > si) & (txs >= n) & (txs < 2 * n) for n in HG_LEVELS}
    upper = {n: (sub & n) != 0 for n in HG_LEVELS if n < SUBLANES}
    go = go_ref[...]
    nt_dims = (((1,), (1,)), ((), ()))

    def tiles(x):
        return [x[i * SUBLANES:(i + 1) * SUBLANES, :] for i in range(n_tiles)]

    def decay_products(rows, cols):
        q = q_ref[0, rows, cols].astype(F32)
        k = k_ref[0, rows, cols].astype(F32)
        qt, kt = tiles(q), tiles(k)
        seg = tiles(g_ref[0, rows, cols])
        tot = list(seg)

        diag = jnp.sum(q * k, axis=-1, keepdims=True)
        level_att = []
        for n in HG_LEVELS:
            x = [None] * n_tiles
            if n < SUBLANES:
                up = upper[n]
                for i in range(n_tiles):
                    e = jnp.exp(jnp.where(up, seg[i], tot[i] - seg[i]))
                    x[i] = jnp.where(up, qt[i], kt[i]) * e
                    fwd = pltpu.roll(tot[i], n, axis=0)
                    sib = fwd if 2 * n == SUBLANES else jnp.where(
                        up, fwd, pltpu.roll(tot[i], SUBLANES - n, axis=0))
                    seg[i] = seg[i] + jnp.where(up, sib, 0.0)
                    tot[i] = tot[i] + sib
            else:
                m = n // SUBLANES
                for b0 in range(0, n_tiles, 2 * m):
                    t_lo, t_hi = tot[b0], tot[b0 + m]
                    for i in range(b0, b0 + m):
                        x[i] = kt[i] * jnp.exp(t_lo - seg[i])
                    for i in range(b0 + m, b0 + 2 * m):
                        x[i] = qt[i] * jnp.exp(seg[i])
                        seg[i] = seg[i] + t_lo
                    tot[b0:b0 + 2 * m] = [t_lo + t_hi] * (2 * m)
            xn = jnp.concatenate(x, axis=0).astype(BF16)
            level_att.append(lax.dot_general(xn, xn, nt_dims, preferred_element_type=F32))

        total = tot[0]
        q_in = jnp.concatenate([qt[i] * jnp.exp(seg[i]) for i in range(n_tiles)], axis=0)
        k_out = jnp.concatenate([kt[i] * jnp.exp(total - seg[i]) for i in range(n_tiles)],
                                axis=0)
        return (diag, level_att, q_in.astype(BF16), k_out.astype(BF16),
                jnp.exp(total[0:1, :]))

    def finish(rows, cols, parts, st):
        diag, level_att, q_in, k_out, decay = parts
        v = v_ref[0, rows, cols]
        att = jnp.where(eye, diag, 0.0)
        for n, a_n in zip(HG_LEVELS, level_att):
            att = jnp.where(pair[n], a_n, att)
        intra = jnp.dot(att.astype(BF16), v, preferred_element_type=F32)
        kv = lax.dot_general(v, k_out, (((0,), (0,)), ((), ())),
                             preferred_element_type=F32)
        inter = lax.dot_general(q_in, st.astype(BF16), nt_dims,
                                preferred_element_type=F32)
        y = inter + intra
        y = y * _row_rms_scale(y) * go
        o_ref[0, rows, cols] = (y * z_ref[0, rows, cols].astype(F32)).astype(o_ref.dtype)
        return st * decay + kv

    state = [st_ref[h] for h in range(hb)]
    pending = None
    for ci in range(tt // c_len):
        rows = slice(ci * c_len, (ci + 1) * c_len)
        for h in range(hb):
            cols = slice(h * HEAD_DIM, (h + 1) * HEAD_DIM)
            parts = decay_products(rows, cols)
            if pending is not None:
                p_rows, p_cols, p_h, p_parts = pending
                state[p_h] = finish(p_rows, p_cols, p_parts, state[p_h])
            pending = (rows, cols, h, parts)
    p_rows, p_cols, p_h, p_parts = pending
    state[p_h] = finish(p_rows, p_cols, p_parts, state[p_h])
    for h in range(hb):
        st_ref[h] = state[h]


def _hgrn(q3, k3, g3, v3, zs3, g_o, tt=256, hb=4):
    b, s, d = q3.shape
    n_groups = d // (HEAD_DIM * hb)
    spec = pl.BlockSpec((1, tt, HEAD_DIM * hb), lambda bi, hg, t: (bi, t, hg))
    return pl.pallas_call(
        functools.partial(_hgrn_kernel, tt=tt, hb=hb),
        grid=(b, n_groups, s // tt),
        in_specs=[spec, spec, spec, spec, spec,
                  pl.BlockSpec((1, HEAD_DIM), lambda bi, hg, t: (0, 0))],
        out_specs=spec,
        out_shape=jax.ShapeDtypeStruct((b, s, d), BF16),
        scratch_shapes=[pltpu.VMEM((hb, HEAD_DIM, HEAD_DIM), F32)],
        compiler_params=_cparams(("parallel", "parallel", "arbitrary")),
        name="hgrn2_scan",
    )(q3, k3, g3, v3, zs3, g_o.reshape(1, HEAD_DIM))


def _out_kernel(og_ref, h_ref, p_ref, wo_ref, wpe_ref, wpg_ref, post_ref, *rest, emit_xn):
    y = jnp.dot(og_ref[...], wo_ref[...], preferred_element_type=F32)
    h1 = h_ref[...] + y * _row_rms_scale(y) * post_ref[...]
    pe = jnp.dot(p_ref[...].astype(BF16), wpe_ref[...], preferred_element_type=F32)
    gate = _sigmoid(jnp.dot(h1.astype(BF16), wpg_ref[...], preferred_element_type=F32))
    h2 = h1 + pe * gate
    if emit_xn:
        pre_ref, h_out_ref, xn_ref = rest
        xn_ref[...] = (h2 * _row_rms_scale(h2) * pre_ref[...]).astype(xn_ref.dtype)
    else:
        (h_out_ref,) = rest
    h_out_ref[...] = h2


def _out_block(og, h, p_all, layer, w_out, w_pe, w_pg, post, pre_next, tm=512):
    t, d = h.shape
    pdim = p_all.shape[-1]
    emit_xn = pre_next is not None
    row = lambda i: (i, 0)
    whole = lambda i: (0, 0)
    const = dict(pipeline_mode=pl.Buffered(1))
    in_specs = [pl.BlockSpec((tm, d), row),
                pl.BlockSpec((tm, d), row),
                pl.BlockSpec((None, tm, pdim), lambda i: (layer, i, 0)),
                pl.BlockSpec((d, d), whole, **const),
                pl.BlockSpec((pdim, d), whole, **const),
                pl.BlockSpec((d, d), whole, **const),
                pl.BlockSpec((1, d), whole)]
    args = [og, h, p_all, w_out, w_pe, w_pg, post.reshape(1, d)]
    out_specs = [pl.BlockSpec((tm, d), row)]
    out_shape = [jax.ShapeDtypeStruct((t, d), F32)]
    if emit_xn:
        in_specs.append(pl.BlockSpec((1, d), whole))
        args.append(pre_next.reshape(1, d))
        out_specs.append(pl.BlockSpec((tm, d), row))
        out_shape.append(jax.ShapeDtypeStruct((t, d), BF16))
    outs = pl.pallas_call(
        functools.partial(_out_kernel, emit_xn=emit_xn),
        grid=(t // tm,),
        in_specs=in_specs,
        out_specs=out_specs,
        out_shape=out_shape,
        compiler_params=_cparams(("parallel",)),
        name=f"out_block_{layer}",
    )(*args)
    return outs if emit_xn else (outs[0], None)


def _fox_layer(xn, b, s, w_in, b_f, g_q, g_k):
    t, d = xn.shape
    n_heads = d // HEAD_DIM
    w = w_in.astype(BF16)
    scale = LOG2E / math.sqrt(HEAD_DIM)
    gq = g_q.reshape(1, HEAD_DIM)
    gk = g_k.reshape(1, HEAD_DIM)
    (q,) = _proj(xn, w[:, 0:d], [gq], [_head_spec()], [BF16],
                 functools.partial(_ep_headnorm, scale=scale), "fox_proj_q")
    (k,) = _proj(xn, w[:, d:2 * d], [gk], [_head_spec()], [BF16],
                 functools.partial(_ep_headnorm, scale=1.0), "fox_proj_k")
    (v,) = _proj(xn, w[:, 2 * d:3 * d], [], [], [BF16], _ep_cast, "fox_proj_v")
    (zs,) = _proj(xn, w[:, 3 * d:4 * d], [], [], [BF16], _ep_silu, "fox_proj_z")

    lanes = HEAD_DIM
    w_f = jnp.zeros((d, lanes), BF16).at[:, :n_heads].set(w[:, 4 * d:])
    b_fp = jnp.zeros((1, lanes), F32).at[0, :n_heads].set(b_f)
    ck = _fgate(xn.reshape(b, s, d), w_f, b_fp, n_heads)

    r3 = lambda a: a.reshape(b, s, d)
    og = _fox_attn(r3(q), r3(k), r3(v), ck, r3(zs), n_heads)
    return og.reshape(t, d)


def _hgrn_layer(xn, b, s, layer, w_in, lb_logits, g_o):
    t, d = xn.shape
    w = w_in.astype(BF16)
    (q,) = _proj(xn, w[:, 0:d], [], [], [BF16], _ep_silu, "hg_proj_q")
    depth = lb_logits.shape[0]
    g, k = _proj(xn, w[:, d:2 * d], [lb_logits],
                 [pl.BlockSpec((depth, PROJ_TN), lambda j, i: (0, j))], [F32, BF16],
                 functools.partial(_ep_forget, layer=layer), "hg_proj_f")
    (v,) = _proj(xn, w[:, 2 * d:3 * d], [], [], [BF16], _ep_cast, "hg_proj_i")
    (zs,) = _proj(xn, w[:, 3 * d:4 * d], [], [], [BF16], _ep_silu, "hg_proj_z")
    r3 = lambda a: a.reshape(b, s, d)
    og = _hgrn(r3(q), r3(k), r3(g), r3(v), r3(zs), g_o)
    return og.reshape(t, d)


def kernel(x, p, w_in_fox, b_f_fox, g_q_fox, g_k_fox, w_out_fox, w_in_hg, lb_logits,
           g_o_hg, w_out_hg, pre_norm, post_norm, w_pe, w_pg):
    b, s, d = x.shape
    depth = p.shape[0]
    t = b * s
    h = x.reshape(t, d)
    p_all = p.reshape(depth, t, p.shape[-1])
    xn = _prenorm(h, pre_norm[0])
    for i in range(depth):
        j = i // 2
        if i % 2 == 0:
            og = _fox_layer(xn, b, s, w_in_fox[j], b_f_fox[j], g_q_fox[j], g_k_fox[j])
            w_out = w_out_fox[j]
        else:
            og = _hgrn_layer(xn, b, s, i, w_in_hg[j], lb_logits, g_o_hg[j])
            w_out = w_out_hg[j]
        pre_next = pre_norm[i + 1] if i + 1 < depth else None
        h, xn = _out_block(og, h, p_all, i, w_out.astype(BF16), w_pe[i].astype(BF16),
                           w_pg[i].astype(BF16), post_norm[i], pre_next)
    return h.reshape(b, s, d)
```

```python
import functools
import math

import jax
import jax.numpy as jnp
from jax import lax
from jax.experimental import pallas as pl
from jax.experimental.pallas import tpu as pltpu

HEAD_DIM = 128
EPS = 1e-6
NEG = -1e30
LOG2E = math.log2(math.e)
ONES_ROWS = 16
HG_CHUNK = 64
HG_LEVELS = (1, 2, 4, 8, 16, 32)
SUBLANES = 8

F32 = jnp.float32
BF16 = jnp.bfloat16

VMEM_LIMIT_BYTES = 56 * 1024 * 1024


def _cparams(semantics):
    return pltpu.CompilerParams(dimension_semantics=semantics,
                                vmem_limit_bytes=VMEM_LIMIT_BYTES)


def _sigmoid(x):
    return 1.0 / (1.0 + jnp.exp(-x))


def _silu(x):
    return x * _sigmoid(x)


def _row_rms_scale(y):
    return lax.rsqrt(jnp.mean(y * y, axis=-1, keepdims=True) + EPS)


def _prenorm_kernel(x_ref, g_ref, o_ref):
    x = x_ref[...]
    o_ref[...] = (x * _row_rms_scale(x) * g_ref[...]).astype(o_ref.dtype)


def _prenorm(x2d, gain, tm=512):
    t, d = x2d.shape
    return pl.pallas_call(
        _prenorm_kernel,
        grid=(t // tm,),
        in_specs=[pl.BlockSpec((tm, d), lambda i: (i, 0)),
                  pl.BlockSpec((1, d), lambda i: (0, 0))],
        out_specs=pl.BlockSpec((tm, d), lambda i: (i, 0)),
        out_shape=jax.ShapeDtypeStruct((t, d), BF16),
        compiler_params=_cparams(("parallel",)),
        name="prenorm",
    )(x2d, gain.reshape(1, d))


def _proj_kernel(a_ref, w_ref, *rest, epilogue, n_extra):
    acc = jnp.dot(a_ref[...], w_ref[...], preferred_element_type=F32)
    epilogue(acc, rest[:n_extra], rest[n_extra:])


PROJ_TM = 1024
PROJ_TN = 1024


def _proj(a, w, seg, n, extras, extra_specs, out_dtypes, epilogue, name,
          tm=PROJ_TM, tn=PROJ_TN):
    t, k = a.shape
    col0 = seg * (n // tn)
    kern = functools.partial(_proj_kernel, epilogue=epilogue, n_extra=len(extras))
    outs = pl.pallas_call(
        kern,
        grid=(n // tn, t // tm),
        in_specs=[pl.BlockSpec((tm, k), lambda j, i: (i, 0)),
                  pl.BlockSpec((k, tn), lambda j, i: (0, col0 + j))] + list(extra_specs),
        out_specs=[pl.BlockSpec((tm, tn), lambda j, i: (i, j)) for _ in out_dtypes],
        out_shape=[jax.ShapeDtypeStruct((t, n), dt) for dt in out_dtypes],
        compiler_params=_cparams(("parallel", "parallel")),
        name=name,
    )(a, w, *extras)
    return outs


def _head_spec():
    return pl.BlockSpec((1, HEAD_DIM), lambda j, i: (0, 0))


def _ep_cast(acc, extras, outs):
    outs[0][...] = acc.astype(outs[0].dtype)


def _ep_silu(acc, extras, outs):
    outs[0][...] = _silu(acc).astype(outs[0].dtype)


def _ep_headnorm(acc, extras, outs, *, scale):
    gain = extras[0][...] * scale
    for h in range(acc.shape[1] // HEAD_DIM):
        cols = slice(h * HEAD_DIM, (h + 1) * HEAD_DIM)
        y = acc[:, cols]
        outs[0][:, cols] = (y * _row_rms_scale(y) * gain).astype(outs[0].dtype)


def _ep_forget(acc, extras, outs, *, layer):
    logits = extras[0][...]
    e = jnp.exp(logits - jnp.max(logits, axis=0, keepdims=True))
    gam = e / jnp.sum(e, axis=0, keepdims=True)
    lb = jnp.sum(gam[:layer + 1], axis=0, keepdims=True) - gam[0:1]
    forget = lb + (1.0 - lb) * _sigmoid(acc)
    outs[0][...] = jnp.log(forget)
    outs[1][...] = (1.0 - forget).astype(outs[1].dtype)


def _fgate_kernel(a_ref, w_ref, b_ref, c_ref, carry_ref, *, tm, n_heads):
    @pl.when(pl.program_id(1) == 0)
    def _():
        carry_ref[...] = jnp.zeros_like(carry_ref)

    f = jnp.dot(a_ref[0], w_ref[...], preferred_element_type=F32) + b_ref[...]
    x = jnp.minimum(f, 0.0) - jnp.log(1.0 + jnp.exp(-jnp.abs(f)))
    rows = lax.broadcasted_iota(jnp.int32, x.shape, 0)
    d = 1
    while d < tm:
        x = x + jnp.where(rows >= d, pltpu.roll(x, d, axis=0), 0.0)
        d *= 2
    x = x + carry_ref[...]
    carry_ref[...] = x[tm - 1:tm, :]
    x2 = x * LOG2E
    for h in range(n_heads):
        c_ref[0, h] = jnp.broadcast_to(x2[:, h:h + 1], x.shape)


def _fgate(xn3, w_f, b_f, n_heads, tm=512):
    b, s, d = xn3.shape
    lanes = w_f.shape[1]
    return pl.pallas_call(
        functools.partial(_fgate_kernel, tm=tm, n_heads=n_heads),
        grid=(b, s // tm),
        in_specs=[pl.BlockSpec((1, tm, d), lambda bi, i: (bi, i, 0)),
                  pl.BlockSpec((d, lanes), lambda bi, i: (0, 0)),
                  pl.BlockSpec((1, lanes), lambda bi, i: (0, 0))],
        out_specs=pl.BlockSpec((1, n_heads, tm, lanes), lambda bi, i: (bi, 0, i, 0)),
        out_shape=jax.ShapeDtypeStruct((b, n_heads, s, lanes), F32),
        scratch_shapes=[pltpu.VMEM((1, lanes), F32)],
        compiler_params=_cparams(("parallel", "arbitrary")),
        name="fox_fgate",
    )(xn3, w_f, b_f)


def _fox_attn_kernel(q_ref, k_ref, v_ref, ck_ref, z_ref, o_ref, acc_ref, s0_ref, s1_ref,
                     kx_ref, *, qb, kb):
    i = pl.program_id(2)
    per_tile = qb // kb
    seq = k_ref.shape[1]
    lane_q = lax.broadcasted_iota(jnp.int32, (qb, HEAD_DIM), 1)
    q = jnp.concatenate(
        [q_ref[0], jnp.where(lane_q < 3, 1.0, 0.0).astype(BF16)], axis=1)

    @pl.when(i == 0)
    def _():
        lane_k = lax.broadcasted_iota(jnp.int32, (kb, HEAD_DIM), 1)

        def extend(j, carry):
            r0 = pl.multiple_of(j * kb, kb)
            c = -ck_ref[0, 0, pl.ds(r0, kb), :]
            hi = c.astype(BF16)
            r1 = c - hi.astype(F32)
            mid = r1.astype(BF16)
            lo = (r1 - mid.astype(F32)).astype(BF16)
            zero = jnp.zeros_like(hi)
            ext = jnp.where(lane_k == 0, hi, jnp.where(lane_k == 1, mid,
                                                       jnp.where(lane_k == 2, lo, zero)))
            kx_ref[pl.ds(r0, kb), 0:HEAD_DIM] = k_ref[0, pl.ds(r0, kb), :]
            kx_ref[pl.ds(r0, kb), HEAD_DIM:2 * HEAD_DIM] = ext
            return carry

        lax.fori_loop(0, seq // kb, extend, 0)

    acc_ref[...] = jnp.zeros_like(acc_ref)
    ones = jnp.ones((ONES_ROWS, kb), BF16)

    def scores(t, s_ref):
        for u in range(per_tile):
            r0 = pl.multiple_of(t * qb + u * kb, kb)
            s_ref[u * kb:(u + 1) * kb, :] = lax.dot_general(
                kx_ref[pl.ds(r0, kb), :], q, (((1,), (1,)), ((), ())),
                preferred_element_type=F32)

    def absorb(t, s_ref, m_old, diagonal):
        for u in range(per_tile):
            r0 = pl.multiple_of(t * qb + u * kb, kb)
            vt = jnp.concatenate([v_ref[0, pl.ds(r0, kb), :].T, ones], axis=0)
            c0 = u * kb if diagonal else 0
            s = s_ref[u * kb:(u + 1) * kb, c0:]
            if diagonal:
                tri = (lax.broadcasted_iota(jnp.int32, (kb, kb), 0)
                       <= lax.broadcasted_iota(jnp.int32, (kb, kb), 1))
                parts = [jnp.where(tri, s[:, :kb], NEG)]
                if s.shape[1] > kb:
                    parts.append(s[:, kb:])
                s = jnp.concatenate(parts, axis=1)
            m_part = m_old[:, c0:]
            m_new = jnp.maximum(m_part, jnp.max(s, axis=0, keepdims=True))
            alpha = jnp.exp2(m_part - m_new)
            p = jnp.exp2(s - m_new).astype(BF16)
            acc_ref[:, c0:] = alpha * acc_ref[:, c0:] + jnp.dot(
                vt, p, preferred_element_type=F32)
            m_old = jnp.concatenate([m_old[:, :c0], m_new], axis=1) if c0 else m_new
        return m_old

    def body(a, m):
        t = 2 * a
        scores(t + 1, s1_ref)
        m = absorb(t, s0_ref, m, diagonal=False)
        scores(t + 2, s0_ref)
        return absorb(t + 1, s1_ref, m, diagonal=False)

    scores(0, s0_ref)
    m = lax.fori_loop(0, i // 2, body, jnp.full((1, qb), NEG, F32))
    odd = lax.rem(i, 2) == 1

    @pl.when(jnp.logical_not(odd))
    def _():
        absorb(i, s0_ref, m, diagonal=True)

    @pl.when(odd)
    def _():
        scores(i, s1_ref)
        m1 = absorb(i - 1, s0_ref, m, diagonal=False)
        absorb(i, s1_ref, m1, diagonal=True)

    l = acc_ref[HEAD_DIM:HEAD_DIM + 1, :]
    o = (acc_ref[0:HEAD_DIM, :] * (1.0 / l)).T
    o_ref[0] = (o * z_ref[0].astype(F32)).astype(o_ref.dtype)


def _fox_attn(q3, k3, v3, ck, zs3, n_heads, qb=1024, kb=512):
    b, s, d = q3.shape
    lanes = ck.shape[-1]
    qspec = pl.BlockSpec((1, qb, HEAD_DIM), lambda bi, h, i: (bi, i, h))
    kvspec = pl.BlockSpec((1, s, HEAD_DIM), lambda bi, h, i: (bi, 0, h))
    return pl.pallas_call(
        functools.partial(_fox_attn_kernel, qb=qb, kb=kb),
        grid=(b, n_heads, s // qb),
        in_specs=[qspec, kvspec, kvspec,
                  pl.BlockSpec((1, 1, s, lanes), lambda bi, h, i: (bi, h, 0, 0)),
                  qspec],
        out_specs=qspec,
        out_shape=jax.ShapeDtypeStruct((b, s, d), BF16),
        scratch_shapes=[pltpu.VMEM((HEAD_DIM + ONES_ROWS, qb), F32),
                        pltpu.VMEM((qb, qb), F32), pltpu.VMEM((qb, qb), F32),
                        pltpu.VMEM((s, 2 * HEAD_DIM), BF16)],
        compiler_params=_cparams(("parallel", "parallel", "arbitrary")),
        name="fox_attn",
    )(q3, k3, v3, ck, zs3)


def _hgrn_kernel(q_ref, k_ref, g_ref, v_ref, z_ref, go_ref, o_ref, st_ref, *, tt, hb):
    c_len = HG_CHUNK

    @pl.when(pl.program_id(2) == 0)
    def _():
        st_ref[...] = jnp.zeros_like(st_ref)

    n_tiles = c_len // SUBLANES
    sub = lax.broadcasted_iota(jnp.int32, (SUBLANES, HEAD_DIM), 0)
    ti = lax.broadcasted_iota(jnp.int32, (c_len, c_len), 0)
    si = lax.broadcasted_iota(jnp.int32, (c_len, c_len), 1)
    txs = ti ^ si
    eye = ti == si
    pair = {n: (ti > si) & (txs >= n) & (txs < 2 * n) for n in HG_LEVELS}
    upper = {n: (sub & n) != 0 for n in HG_LEVELS if n < SUBLANES}
    go = go_ref[...]
    nt_dims = (((1,), (1,)), ((), ()))

    def tiles(x):
        return [x[i * SUBLANES:(i + 1) * SUBLANES, :] for i in range(n_tiles)]

    def decay_products(rows, cols):
        q = q_ref[0, rows, cols].astype(F32)
        k = k_ref[0, rows, cols].astype(F32)
        qt, kt = tiles(q), tiles(k)
        seg = tiles(g_ref[0, rows, cols])
        tot = list(seg)

        diag = jnp.sum(q * k, axis=-1, keepdims=True)
        level_att = []
        for n in HG_LEVELS:
            x = [None] * n_tiles
            if n < SUBLANES:
                up = upper[n]
                for i in range(n_tiles):
                    e = jnp.exp(jnp.where(up, seg[i], tot[i] - seg[i]))
                    x[i] = jnp.where(up, qt[i], kt[i]) * e
                    fwd = pltpu.roll(tot[i], n, axis=0)
                    sib = fwd if 2 * n == SUBLANES else jnp.where(
                        up, fwd, pltpu.roll(tot[i], SUBLANES - n, axis=0))
                    seg[i] = seg[i] + jnp.where(up, sib, 0.0)
                    tot[i] = tot[i] + sib
            else:
                m = n // SUBLANES
                for b0 in range(0, n_tiles, 2 * m):
                    t_lo, t_hi = tot[b0], tot[b0 + m]
                    for i in range(b0, b0 + m):
                        x[i] = kt[i] * jnp.exp(t_lo - seg[i])
                    for i in range(b0 + m, b0 + 2 * m):
                        x[i] = qt[i] * jnp.exp(seg[i])
                        seg[i] = seg[i] + t_lo
                    tot[b0:b0 + 2 * m] = [t_lo + t_hi] * (2 * m)
            xn = jnp.concatenate(x, axis=0).astype(BF16)
            level_att.append(lax.dot_general(xn, xn, nt_dims, preferred_element_type=F32))

        total = tot[0]
        q_in = jnp.concatenate([qt[i] * jnp.exp(seg[i]) for i in range(n_tiles)], axis=0)
        k_out = jnp.concatenate([kt[i] * jnp.exp(total - seg[i]) for i in range(n_tiles)],
                                axis=0)
        return (diag, level_att, q_in.astype(BF16), k_out.astype(BF16),
                jnp.exp(total[0:1, :]))

    def finish(rows, cols, parts, st):
        diag, level_att, q_in, k_out, decay = parts
        v = v_ref[0, rows, cols]
        att = jnp.where(eye, diag, 0.0)
        for n, a_n in zip(HG_LEVELS, level_att):
            att = jnp.where(pair[n], a_n, att)
        intra = jnp.dot(att.astype(BF16), v, preferred_element_type=F32)
        kv = lax.dot_general(v, k_out, (((0,), (0,)), ((), ())),
                             preferred_element_type=F32)
        inter = lax.dot_general(q_in, st.astype(BF16), nt_dims,
                                preferred_element_type=F32)
        y = inter + intra
        y = y * _row_rms_scale(y) * go
        o_ref[0, rows, cols] = (y * z_ref[0, rows, cols].astype(F32)).astype(o_ref.dtype)
        return st * decay + kv

    state = [st_ref[h] for h in range(hb)]
    pending = None
    for ci in range(tt // c_len):
        rows = slice(ci * c_len, (ci + 1) * c_len)
        for h in range(hb):
            cols = slice(h * HEAD_DIM, (h + 1) * HEAD_DIM)
            parts = decay_products(rows, cols)
            if pending is not None:
                p_rows, p_cols, p_h, p_parts = pending
                state[p_h] = finish(p_rows, p_cols, p_parts, state[p_h])
            pending = (rows, cols, h, parts)
    p_rows, p_cols, p_h, p_parts = pending
    state[p_h] = finish(p_rows, p_cols, p_parts, state[p_h])
    for h in range(hb):
        st_ref[h] = state[h]


def _hgrn(q3, k3, g3, v3, zs3, g_o, tt=256, hb=4):
    b, s, d = q3.shape
    n_groups = d // (HEAD_DIM * hb)
    spec = pl.BlockSpec((1, tt, HEAD_DIM * hb), lambda bi, hg, t: (bi, t, hg))
    return pl.pallas_call(
        functools.partial(_hgrn_kernel, tt=tt, hb=hb),
        grid=(b, n_groups, s // tt),
        in_specs=[spec, spec, spec, spec, spec,
                  pl.BlockSpec((1, HEAD_DIM), lambda bi, hg, t: (0, 0))],
        out_specs=spec,
        out_shape=jax.ShapeDtypeStruct((b, s, d), BF16),
        scratch_shapes=[pltpu.VMEM((hb, HEAD_DIM, HEAD_DIM), F32)],
        compiler_params=_cparams(("parallel", "parallel", "arbitrary")),
        name="hgrn2_scan",
    )(q3, k3, g3, v3, zs3, g_o.reshape(1, HEAD_DIM))


def _out_kernel(og_ref, h_ref, p_ref, wo_ref, wpe_ref, wpg_ref, post_ref, *rest, emit_xn):
    y = jnp.dot(og_ref[...], wo_ref[...], preferred_element_type=F32)
    h1 = h_ref[...] + y * _row_rms_scale(y) * post_ref[...]
    pe = jnp.dot(p_ref[...].astype(BF16), wpe_ref[...], preferred_element_type=F32)
    gate = _sigmoid(jnp.dot(h1.astype(BF16), wpg_ref[...], preferred_element_type=F32))
    h2 = h1 + pe * gate
    if emit_xn:
        pre_ref, h_out_ref, xn_ref = rest
        xn_ref[...] = (h2 * _row_rms_scale(h2) * pre_ref[...]).astype(xn_ref.dtype)
    else:
        (h_out_ref,) = rest
    h_out_ref[...] = h2


def _out_block(og, h, p_all, layer, w_out, w_pe, w_pg, post, pre_next, tm=512):
    t, d = h.shape
    pdim = p_all.shape[-1]
    emit_xn = pre_next is not None
    row = lambda i: (i, 0)
    whole = lambda i: (0, 0)
    const = dict(pipeline_mode=pl.Buffered(1))
    in_specs = [pl.BlockSpec((tm, d), row),
                pl.BlockSpec((tm, d), row),
                pl.BlockSpec((None, tm, pdim), lambda i: (layer, i, 0)),
                pl.BlockSpec((d, d), whole, **const),
                pl.BlockSpec((pdim, d), whole, **const),
                pl.BlockSpec((d, d), whole, **const),
                pl.BlockSpec((1, d), whole)]
    args = [og, h, p_all, w_out, w_pe, w_pg, post.reshape(1, d)]
    out_specs = [pl.BlockSpec((tm, d), row)]
    out_shape = [jax.ShapeDtypeStruct((t, d), F32)]
    if emit_xn:
        in_specs.append(pl.BlockSpec((1, d), whole))
        args.append(pre_next.reshape(1, d))
        out_specs.append(pl.BlockSpec((tm, d), row))
        out_shape.append(jax.ShapeDtypeStruct((t, d), BF16))
    outs = pl.pallas_call(
        functools.partial(_out_kernel, emit_xn=emit_xn),
        grid=(t // tm,),
        in_specs=in_specs,
        out_specs=out_specs,
        out_shape=out_shape,
        compiler_params=_cparams(("parallel",)),
        name=f"out_block_{layer}",
    )(*args)
    return outs if emit_xn else (outs[0], None)


def _fox_layer(xn, b, s, w_in, b_f, g_q, g_k):
    t, d = xn.shape
    n_heads = d // HEAD_DIM
    w = w_in.astype(BF16)
    scale = LOG2E / math.sqrt(HEAD_DIM)
    gq = g_q.reshape(1, HEAD_DIM)
    gk = g_k.reshape(1, HEAD_DIM)
    (q,) = _proj(xn, w, 0, d, [gq], [_head_spec()], [BF16],
                 functools.partial(_ep_headnorm, scale=scale), "fox_proj_q")
    (k,) = _proj(xn, w, 1, d, [gk], [_head_spec()], [BF16],
                 functools.partial(_ep_headnorm, scale=1.0), "fox_proj_k")
    (v,) = _proj(xn, w, 2, d, [], [], [BF16], _ep_cast, "fox_proj_v")
    (zs,) = _proj(xn, w, 3, d, [], [], [BF16], _ep_silu, "fox_proj_z")

    lanes = HEAD_DIM
    w_f = jnp.zeros((d, lanes), BF16).at[:, :n_heads].set(w[:, 4 * d:])
    b_fp = jnp.zeros((1, lanes), F32).at[0, :n_heads].set(b_f)
    ck = _fgate(xn.reshape(b, s, d), w_f, b_fp, n_heads)

    r3 = lambda a: a.reshape(b, s, d)
    og = _fox_attn(r3(q), r3(k), r3(v), ck, r3(zs), n_heads)
    return og.reshape(t, d)


def _hgrn_layer(xn, b, s, layer, w_in, lb_logits, g_o):
    t, d = xn.shape
    w = w_in.astype(BF16)
    (q,) = _proj(xn, w, 0, d, [], [], [BF16], _ep_silu, "hg_proj_q")
    depth = lb_logits.shape[0]
    g, k = _proj(xn, w, 1, d, [lb_logits],
                 [pl.BlockSpec((depth, PROJ_TN), lambda j, i: (0, j))], [F32, BF16],
                 functools.partial(_ep_forget, layer=layer), "hg_proj_f")
    (v,) = _proj(xn, w, 2, d, [], [], [BF16], _ep_cast, "hg_proj_i")
    (zs,) = _proj(xn, w, 3, d, [], [], [BF16], _ep_silu, "hg_proj_z")
    r3 = lambda a: a.reshape(b, s, d)
    og = _hgrn(r3(q), r3(k), r3(g), r3(v), r3(zs), g_o)
    return og.reshape(t, d)


def kernel(x, p, w_in_fox, b_f_fox, g_q_fox, g_k_fox, w_out_fox, w_in_hg, lb_logits,
           g_o_hg, w_out_hg, pre_norm, post_norm, w_pe, w_pg):
    b, s, d = x.shape
    depth = p.shape[0]
    t = b * s
    h = x.reshape(t, d)
    p_all = p.reshape(depth, t, p.shape[-1])
    xn = _prenorm(h, pre_norm[0])
    for i in range(depth):
        j = i // 2
        if i % 2 == 0:
            og = _fox_layer(xn, b, s, w_in_fox[j], b_f_fox[j], g_q_fox[j], g_k_fox[j])
            w_out = w_out_fox[j]
        else:
            og = _hgrn_layer(xn, b, s, i, w_in_hg[j], lb_logits, g_o_hg[j])
            w_out = w_out_hg[j]
        pre_next = pre_norm[i + 1] if i + 1 < depth else None
        h, xn = _out_block(og, h, p_all, i, w_out.astype(BF16), w_pe[i].astype(BF16),
                           w_pg[i].astype(BF16), post_norm[i], pre_next)
    return h.reshape(b, s, d)
```

```python
import functools
import math

import jax
import jax.numpy as jnp
from jax import lax
from jax.experimental import pallas as pl
from jax.experimental.pallas import tpu as pltpu

HEAD_DIM = 128
EPS = 1e-6
NEG = -1e30
LOG2E = math.log2(math.e)
ONES_ROWS = 16
HG_CHUNK = 64
HG_LEVELS = (1, 2, 4, 8, 16, 32)
SUBLANES = 8

F32 = jnp.float32
BF16 = jnp.bfloat16

VMEM_LIMIT_BYTES = 56 * 1024 * 1024


def _cparams(semantics):
    return pltpu.CompilerParams(dimension_semantics=semantics,
                                vmem_limit_bytes=VMEM_LIMIT_BYTES)


def _sigmoid(x):
    return 1.0 / (1.0 + jnp.exp(-x))


def _silu(x):
    return x * _sigmoid(x)


def _row_rms_scale(y):
    return lax.rsqrt(jnp.mean(y * y, axis=-1, keepdims=True) + EPS)


def _prenorm_kernel(x_ref, g_ref, o_ref):
    x = x_ref[...]
    o_ref[...] = (x * _row_rms_scale(x) * g_ref[...]).astype(o_ref.dtype)


def _prenorm(x2d, gain, tm=512):
    t, d = x2d.shape
    return pl.pallas_call(
        _prenorm_kernel,
        grid=(t // tm,),
        in_specs=[pl.BlockSpec((tm, d), lambda i: (i, 0)),
                  pl.BlockSpec((1, d), lambda i: (0, 0))],
        out_specs=pl.BlockSpec((tm, d), lambda i: (i, 0)),
        out_shape=jax.ShapeDtypeStruct((t, d), BF16),
        compiler_params=_cparams(("parallel",)),
        name="prenorm",
    )(x2d, gain.reshape(1, d))


def _proj_kernel(a_ref, w_ref, *rest, epilogue, n_extra):
    wb_ref = rest[-1]

    @pl.when(pl.program_id(1) == 0)
    def _():
        wb_ref[...] = w_ref[...].astype(wb_ref.dtype)

    acc = jnp.dot(a_ref[...], wb_ref[...], preferred_element_type=F32)
    epilogue(acc, rest[:n_extra], rest[n_extra:-1])


PROJ_TM = 1024
PROJ_TN = 1024


def _proj(a, w, seg, n, extras, extra_specs, out_dtypes, epilogue, name,
          tm=PROJ_TM, tn=PROJ_TN):
    t, k = a.shape
    col0 = seg * (n // tn)
    kern = functools.partial(_proj_kernel, epilogue=epilogue, n_extra=len(extras))
    outs = pl.pallas_call(
        kern,
        grid=(n // tn, t // tm),
        in_specs=[pl.BlockSpec((tm, k), lambda j, i: (i, 0)),
                  pl.BlockSpec((k, tn), lambda j, i: (0, col0 + j))] + list(extra_specs),
        out_specs=[pl.BlockSpec((tm, tn), lambda j, i: (i, j)) for _ in out_dtypes],
        out_shape=[jax.ShapeDtypeStruct((t, n), dt) for dt in out_dtypes],
        scratch_shapes=[pltpu.VMEM((k, tn), a.dtype)],
        compiler_params=_cparams(("parallel", "arbitrary")),
        name=name,
    )(a, w, *extras)
    return outs


def _head_spec():
    return pl.BlockSpec((1, HEAD_DIM), lambda j, i: (0, 0))


def _ep_cast(acc, extras, outs):
    outs[0][...] = acc.astype(outs[0].dtype)


def _ep_silu(acc, extras, outs):
    outs[0][...] = _silu(acc).astype(outs[0].dtype)


def _ep_headnorm(acc, extras, outs, *, scale):
    gain = extras[0][...] * scale
    for h in range(acc.shape[1] // HEAD_DIM):
        cols = slice(h * HEAD_DIM, (h + 1) * HEAD_DIM)
        y = acc[:, cols]
        outs[0][:, cols] = (y * _row_rms_scale(y) * gain).astype(outs[0].dtype)


def _ep_forget(acc, extras, outs, *, layer):
    logits = extras[0][...]
    e = jnp.exp(logits - jnp.max(logits, axis=0, keepdims=True))
    gam = e / jnp.sum(e, axis=0, keepdims=True)
    lb = jnp.sum(gam[:layer + 1], axis=0, keepdims=True) - gam[0:1]
    forget = lb + (1.0 - lb) * _sigmoid(acc)
    outs[0][...] = jnp.log(forget)
    outs[1][...] = (1.0 - forget).astype(outs[1].dtype)


def _fgate_kernel(a_ref, w_ref, b_ref, c_ref, carry_ref, *, tm):
    @pl.when(pl.program_id(1) == 0)
    def _():
        carry_ref[...] = jnp.zeros_like(carry_ref)

    f = jnp.dot(a_ref[0], w_ref[...], preferred_element_type=F32) + b_ref[...]
    x = jnp.minimum(f, 0.0) - jnp.log(1.0 + jnp.exp(-jnp.abs(f)))
    rows = lax.broadcasted_iota(jnp.int32, x.shape, 0)
    d = 1
    while d < tm:
        x = x + jnp.where(rows >= d, pltpu.roll(x, d, axis=0), 0.0)
        d *= 2
    x = x + carry_ref[...]
    carry_ref[...] = x[tm - 1:tm, :]
    hi, mid, lo = _split3(-x * LOG2E)
    lanes = x.shape[1]
    c_ref[0, :, 0:lanes] = hi
    c_ref[0, :, lanes:2 * lanes] = mid
    c_ref[0, :, 2 * lanes:3 * lanes] = lo


def _split3(x):
    hi = x.astype(BF16)
    r1 = x - hi.astype(F32)
    mid = r1.astype(BF16)
    lo = (r1 - mid.astype(F32)).astype(BF16)
    return hi, mid, lo


def _fgate(xn3, w_f, b_f, tm=512):
    b, s, d = xn3.shape
    lanes = w_f.shape[1]
    return pl.pallas_call(
        functools.partial(_fgate_kernel, tm=tm),
        grid=(b, s // tm),
        in_specs=[pl.BlockSpec((1, tm, d), lambda bi, i: (bi, i, 0)),
                  pl.BlockSpec((d, lanes), lambda bi, i: (0, 0)),
                  pl.BlockSpec((1, lanes), lambda bi, i: (0, 0))],
        out_specs=pl.BlockSpec((1, tm, 3 * lanes), lambda bi, i: (bi, i, 0)),
        out_shape=jax.ShapeDtypeStruct((b, s, 3 * lanes), BF16),
        scratch_shapes=[pltpu.VMEM((1, lanes), F32)],
        compiler_params=_cparams(("parallel", "arbitrary")),
        name="fox_fgate",
    )(xn3, w_f, b_f)


def _fox_attn_kernel(q_ref, k_ref, v_ref, ck_ref, z_ref, o_ref, acc_ref, s0_ref, s1_ref,
                     kx_ref, *, qb, kb):
    i = pl.program_id(2)
    per_tile = qb // kb
    seq = k_ref.shape[1]
    lane_q = lax.broadcasted_iota(jnp.int32, (qb, HEAD_DIM), 1)
    q = jnp.concatenate(
        [q_ref[0], jnp.where(lane_q < 3, 1.0, 0.0).astype(BF16)], axis=1)

    @pl.when(i == 0)
    def _():
        n_in = ck_ref.shape[2]
        lanes = n_in // 3
        rr = lax.broadcasted_iota(jnp.int32, (n_in, HEAD_DIM), 0)
        cc = lax.broadcasted_iota(jnp.int32, (n_in, HEAD_DIM), 1)
        pick = jnp.where((cc < 3) & (rr == cc * lanes + pl.program_id(1)),
                         1.0, 0.0).astype(BF16)

        def extend(j, carry):
            r0 = pl.multiple_of(j * kb, kb)
            ext = jnp.dot(ck_ref[0, pl.ds(r0, kb), :], pick, preferred_element_type=F32)
            kx_ref[pl.ds(r0, kb), 0:HEAD_DIM] = k_ref[0, pl.ds(r0, kb), :]
            kx_ref[pl.ds(r0, kb), HEAD_DIM:2 * HEAD_DIM] = ext.astype(BF16)
            return carry

        lax.fori_loop(0, seq // kb, extend, 0)

    acc_ref[...] = jnp.zeros_like(acc_ref)
    ones = jnp.ones((ONES_ROWS, kb), BF16)

    def scores(t, s_ref):
        for u in range(per_tile):
            r0 = pl.multiple_of(t * qb + u * kb, kb)
            s_ref[u * kb:(u + 1) * kb, :] = lax.dot_general(
                kx_ref[pl.ds(r0, kb), :], q, (((1,), (1,)), ((), ())),
                preferred_element_type=F32)

    def absorb(t, s_ref, m_old, diagonal):
        for u in range(per_tile):
            r0 = pl.multiple_of(t * qb + u * kb, kb)
            vt = jnp.concatenate([v_ref[0, pl.ds(r0, kb), :].T, ones], axis=0)
            c0 = u * kb if diagonal else 0
            s = s_ref[u * kb:(u + 1) * kb, c0:]
            if diagonal:
                tri = (lax.broadcasted_iota(jnp.int32, (kb, kb), 0)
                       <= lax.broadcasted_iota(jnp.int32, (kb, kb), 1))
                parts = [jnp.where(tri, s[:, :kb], NEG)]
                if s.shape[1] > kb:
                    parts.append(s[:, kb:])
                s = jnp.concatenate(parts, axis=1)
            m_part = m_old[:, c0:]
            m_new = jnp.maximum(m_part, jnp.max(s, axis=0, keepdims=True))
            alpha = jnp.exp2(m_part - m_new)
            p = jnp.exp2(s - m_new).astype(BF16)
            acc_ref[:, c0:] = alpha * acc_ref[:, c0:] + jnp.dot(
                vt, p, preferred_element_type=F32)
            m_old = jnp.concatenate([m_old[:, :c0], m_new], axis=1) if c0 else m_new
        return m_old

    def body(a, m):
        t = 2 * a
        scores(t + 1, s1_ref)
        m = absorb(t, s0_ref, m, diagonal=False)
        scores(t + 2, s0_ref)
        return absorb(t + 1, s1_ref, m, diagonal=False)

    scores(0, s0_ref)
    m = lax.fori_loop(0, i // 2, body, jnp.full((1, qb), NEG, F32))
    odd = lax.rem(i, 2) == 1

    @pl.when(jnp.logical_not(odd))
    def _():
        absorb(i, s0_ref, m, diagonal=True)

    @pl.when(odd)
    def _():
        scores(i, s1_ref)
        m1 = absorb(i - 1, s0_ref, m, diagonal=False)
        absorb(i, s1_ref, m1, diagonal=True)

    l = acc_ref[HEAD_DIM:HEAD_DIM + 1, :]
    o = (acc_ref[0:HEAD_DIM, :] * (1.0 / l)).T
    o_ref[0] = (o * z_ref[0].astype(F32)).astype(o_ref.dtype)


def _fox_attn(q3, k3, v3, ck, zs3, n_heads, qb=1024, kb=512):
    b, s, d = q3.shape
    qspec = pl.BlockSpec((1, qb, HEAD_DIM), lambda bi, h, i: (bi, i, h))
    kvspec = pl.BlockSpec((1, s, HEAD_DIM), lambda bi, h, i: (bi, 0, h))
    return pl.pallas_call(
        functools.partial(_fox_attn_kernel, qb=qb, kb=kb),
        grid=(b, n_heads, s // qb),
        in_specs=[qspec, kvspec, kvspec,
                  pl.BlockSpec((1, s, ck.shape[-1]), lambda bi, h, i: (bi, 0, 0)),
                  qspec],
        out_specs=qspec,
        out_shape=jax.ShapeDtypeStruct((b, s, d), BF16),
        scratch_shapes=[pltpu.VMEM((HEAD_DIM + ONES_ROWS, qb), F32),
                        pltpu.VMEM((qb, qb), F32), pltpu.VMEM((qb, qb), F32),
                        pltpu.VMEM((s, 2 * HEAD_DIM), BF16)],
        compiler_params=_cparams(("parallel", "parallel", "arbitrary")),
        name="fox_attn",
    )(q3, k3, v3, ck, zs3)


def _hgrn_kernel(q_ref, k_ref, g_ref, v_ref, z_ref, go_ref, o_ref, st_ref, *, tt, hb):
    c_len = HG_CHUNK

    @pl.when(pl.program_id(2) == 0)
    def _():
        st_ref[...] = jnp.zeros_like(st_ref)

    n_tiles = c_len // SUBLANES
    sub = lax.broadcasted_iota(jnp.int32, (SUBLANES, HEAD_DIM), 0)
    ti = lax.broadcasted_iota(jnp.int32, (c_len, c_len), 0)
    si = lax.broadcasted_iota(jnp.int32, (c_len, c_len), 1)
    txs = ti ^ si
    eye = ti == si
    pair = {n: (ti > si) & (txs >= n) & (txs < 2 * n) for n in HG_LEVELS}
    upper = {n: (sub & n) != 0 for n in HG_LEVELS if n < SUBLANES}
    go = go_ref[...]
    nt_dims = (((1,), (1,)), ((), ()))

    def tiles(x):
        return [x[i * SUBLANES:(i + 1) * SUBLANES, :] for i in range(n_tiles)]

    def decay_products(rows, cols):
        q = q_ref[0, rows, cols].astype(F32)
        k = k_ref[0, rows, cols].astype(F32)
        qt, kt = tiles(q), tiles(k)
        seg = tiles(g_ref[0, rows, cols])
        tot = list(seg)

        diag = jnp.sum(q * k, axis=-1, keepdims=True)
        level_att = []
        for n in HG_LEVELS:
            x = [None] * n_tiles
            if n < SUBLANES:
                up = upper[n]
                for i in range(n_tiles):
                    e = jnp.exp(jnp.where(up, seg[i], tot[i] - seg[i]))
                    x[i] = jnp.where(up, qt[i], kt[i]) * e
                    fwd = pltpu.roll(tot[i], n, axis=0)
                    sib = fwd if 2 * n == SUBLANES else jnp.where(
                        up, fwd, pltpu.roll(tot[i], SUBLANES - n, axis=0))
                    seg[i] = seg[i] + jnp.where(up, sib, 0.0)
                    tot[i] = tot[i] + sib
            else:
                m = n // SUBLANES
                for b0 in range(0, n_tiles, 2 * m):
                    t_lo, t_hi = tot[b0], tot[b0 + m]
                    for i in range(b0, b0 + m):
                        x[i] = kt[i] * jnp.exp(t_lo - seg[i])
                    for i in range(b0 + m, b0 + 2 * m):
                        x[i] = qt[i] * jnp.exp(seg[i])
                        seg[i] = seg[i] + t_lo
                    tot[b0:b0 + 2 * m] = [t_lo + t_hi] * (2 * m)
            xn = jnp.concatenate(x, axis=0).astype(BF16)
            level_att.append(lax.dot_general(xn, xn, nt_dims, preferred_element_type=F32))

        total = tot[0]
        q_in = jnp.concatenate([qt[i] * jnp.exp(seg[i]) for i in range(n_tiles)], axis=0)
        k_out = jnp.concatenate([kt[i] * jnp.exp(total - seg[i]) for i in range(n_tiles)],
                                axis=0)
        return (diag, level_att, q_in.astype(BF16), k_out.astype(BF16),
                jnp.exp(total[0:1, :]))

    def finish(rows, cols, parts, st):
        diag, level_att, q_in, k_out, decay = parts
        v = v_ref[0, rows, cols]
        att = jnp.where(eye, diag, 0.0)
        for n, a_n in zip(HG_LEVELS, level_att):
            att = jnp.where(pair[n], a_n, att)
        intra = jnp.dot(att.astype(BF16), v, preferred_element_type=F32)
        kv = lax.dot_general(v, k_out, (((0,), (0,)), ((), ())),
                             preferred_element_type=F32)
        inter = lax.dot_general(q_in, st.astype(BF16), nt_dims,
                                preferred_element_type=F32)
        y = inter + intra
        y = y * _row_rms_scale(y) * go
        o_ref[0, rows, cols] = (y * z_ref[0, rows, cols].astype(F32)).astype(o_ref.dtype)
        return st * decay + kv

    state = [st_ref[h] for h in range(hb)]
    pending = None
    for ci in range(tt // c_len):
        rows = slice(ci * c_len, (ci + 1) * c_len)
        for h in range(hb):
            cols = slice(h * HEAD_DIM, (h + 1) * HEAD_DIM)
            parts = decay_products(rows, cols)
            if pending is not None:
                p_rows, p_cols, p_h, p_parts = pending
                state[p_h] = finish(p_rows, p_cols, p_parts, state[p_h])
            pending = (rows, cols, h, parts)
    p_rows, p_cols, p_h, p_parts = pending
    state[p_h] = finish(p_rows, p_cols, p_parts, state[p_h])
    for h in range(hb):
        st_ref[h] = state[h]


def _hgrn(q3, k3, g3, v3, zs3, g_o, tt=256, hb=4):
    b, s, d = q3.shape
    n_groups = d // (HEAD_DIM * hb)
    spec = pl.BlockSpec((1, tt, HEAD_DIM * hb), lambda bi, hg, t: (bi, t, hg))
    return pl.pallas_call(
        functools.partial(_hgrn_kernel, tt=tt, hb=hb),
        grid=(b, n_groups, s // tt),
        in_specs=[spec, spec, spec, spec, spec,
                  pl.BlockSpec((1, HEAD_DIM), lambda bi, hg, t: (0, 0))],
        out_specs=spec,
        out_shape=jax.ShapeDtypeStruct((b, s, d), BF16),
        scratch_shapes=[pltpu.VMEM((hb, HEAD_DIM, HEAD_DIM), F32)],
        compiler_params=_cparams(("parallel", "parallel", "arbitrary")),
        name="hgrn2_scan",
    )(q3, k3, g3, v3, zs3, g_o.reshape(1, HEAD_DIM))


def _out_kernel(og_ref, h_ref, p_ref, wo_ref, wpe_ref, wpg_ref, post_ref, *rest, emit_xn):
    y = jnp.dot(og_ref[...], wo_ref[...], preferred_element_type=F32)
    h1 = h_ref[...] + y * _row_rms_scale(y) * post_ref[...]
    pe = jnp.dot(p_ref[...].astype(BF16), wpe_ref[...], preferred_element_type=F32)
    gate = _sigmoid(jnp.dot(h1.astype(BF16), wpg_ref[...], preferred_element_type=F32))
    h2 = h1 + pe * gate
    if emit_xn:
        pre_ref, h_out_ref, xn_ref = rest
        xn_ref[...] = (h2 * _row_rms_scale(h2) * pre_ref[...]).astype(xn_ref.dtype)
    else:
        (h_out_ref,) = rest
    h_out_ref[...] = h2


def _out_block(og, h, p_all, layer, w_out, w_pe, w_pg, post, pre_next, tm=512):
    t, d = h.shape
    pdim = p_all.shape[-1]
    emit_xn = pre_next is not None
    row = lambda i: (i, 0)
    whole = lambda i: (0, 0)
    const = dict(pipeline_mode=pl.Buffered(1))
    in_specs = [pl.BlockSpec((tm, d), row),
                pl.BlockSpec((tm, d), row),
                pl.BlockSpec((None, tm, pdim), lambda i: (layer, i, 0)),
                pl.BlockSpec((d, d), whole, **const),
                pl.BlockSpec((pdim, d), whole, **const),
                pl.BlockSpec((d, d), whole, **const),
                pl.BlockSpec((1, d), whole)]
    args = [og, h, p_all, w_out, w_pe, w_pg, post.reshape(1, d)]
    out_specs = [pl.BlockSpec((tm, d), row)]
    out_shape = [jax.ShapeDtypeStruct((t, d), F32)]
    if emit_xn:
        in_specs.append(pl.BlockSpec((1, d), whole))
        args.append(pre_next.reshape(1, d))
        out_specs.append(pl.BlockSpec((tm, d), row))
        out_shape.append(jax.ShapeDtypeStruct((t, d), BF16))
    outs = pl.pallas_call(
        functools.partial(_out_kernel, emit_xn=emit_xn),
        grid=(t // tm,),
        in_specs=in_specs,
        out_specs=out_specs,
        out_shape=out_shape,
        compiler_params=_cparams(("parallel",)),
        name=f"out_block_{layer}",
    )(*args)
    return outs if emit_xn else (outs[0], None)


def _fox_layer(xn, b, s, w_in, b_f, g_q, g_k):
    t, d = xn.shape
    n_heads = d // HEAD_DIM
    w = w_in
    scale = LOG2E / math.sqrt(HEAD_DIM)
    gq = g_q.reshape(1, HEAD_DIM)
    gk = g_k.reshape(1, HEAD_DIM)
    (q,) = _proj(xn, w, 0, d, [gq], [_head_spec()], [BF16],
                 functools.partial(_ep_headnorm, scale=scale), "fox_proj_q")
    (k,) = _proj(xn, w, 1, d, [gk], [_head_spec()], [BF16],
                 functools.partial(_ep_headnorm, scale=1.0), "fox_proj_k")
    (v,) = _proj(xn, w, 2, d, [], [], [BF16], _ep_cast, "fox_proj_v")
    (zs,) = _proj(xn, w, 3, d, [], [], [BF16], _ep_silu, "fox_proj_z")

    lanes = HEAD_DIM
    w_f = jnp.zeros((d, lanes), BF16).at[:, :n_heads].set(w[:, 4 * d:].astype(BF16))
    b_fp = jnp.zeros((1, lanes), F32).at[0, :n_heads].set(b_f)
    ck = _fgate(xn.reshape(b, s, d), w_f, b_fp)

    r3 = lambda a: a.reshape(b, s, d)
    og = _fox_attn(r3(q), r3(k), r3(v), ck, r3(zs), n_heads)
    return og.reshape(t, d)


def _hgrn_layer(xn, b, s, layer, w_in, lb_logits, g_o):
    t, d = xn.shape
    w = w_in
    (q,) = _proj(xn, w, 0, d, [], [], [BF16], _ep_silu, "hg_proj_q")
    depth = lb_logits.shape[0]
    g, k = _proj(xn, w, 1, d, [lb_logits],
                 [pl.BlockSpec((depth, PROJ_TN), lambda j, i: (0, j))], [F32, BF16],
                 functools.partial(_ep_forget, layer=layer), "hg_proj_f")
    (v,) = _proj(xn, w, 2, d, [], [], [BF16], _ep_cast, "hg_proj_i")
    (zs,) = _proj(xn, w, 3, d, [], [], [BF16], _ep_silu, "hg_proj_z")
    r3 = lambda a: a.reshape(b, s, d)
    og = _hgrn(r3(q), r3(k), r3(g), r3(v), r3(zs), g_o)
    return og.reshape(t, d)


def kernel(x, p, w_in_fox, b_f_fox, g_q_fox, g_k_fox, w_out_fox, w_in_hg, lb_logits,
           g_o_hg, w_out_hg, pre_norm, post_norm, w_pe, w_pg):
    b, s, d = x.shape
    depth = p.shape[0]
    t = b * s
    h = x.reshape(t, d)
    p_all = p.reshape(depth, t, p.shape[-1])
    xn = _prenorm(h, pre_norm[0])
    for i in range(depth):
        j = i // 2
        if i % 2 == 0:
            og = _fox_layer(xn, b, s, w_in_fox[j], b_f_fox[j], g_q_fox[j], g_k_fox[j])
            w_out = w_out_fox[j]
        else:
            og = _hgrn_layer(xn, b, s, i, w_in_hg[j], lb_logits, g_o_hg[j])
            w_out = w_out_hg[j]
        pre_next = pre_norm[i + 1] if i + 1 < depth else None
        h, xn = _out_block(og, h, p_all, i, w_out.astype(BF16), w_pe[i].astype(BF16),
                           w_pg[i].astype(BF16), post_norm[i], pre_next)
    return h.reshape(b, s, d)
```

```python
import functools
import math

import jax
import jax.numpy as jnp
from jax import lax
from jax.experimental import pallas as pl
from jax.experimental.pallas import tpu as pltpu

HEAD_DIM = 128
EPS = 1e-6
NEG = -1e30
LOG2E = math.log2(math.e)
ONES_ROWS = 16
HG_CHUNK = 64
HG_LEVELS = (1, 2, 4, 8, 16, 32)
SUBLANES = 8

F32 = jnp.float32
BF16 = jnp.bfloat16

VMEM_LIMIT_BYTES = 56 * 1024 * 1024


def _cparams(semantics):
    return pltpu.CompilerParams(dimension_semantics=semantics,
                                vmem_limit_bytes=VMEM_LIMIT_BYTES)


def _sigmoid(x):
    return 1.0 / (1.0 + jnp.exp(-x))


def _silu(x):
    return x * _sigmoid(x)


def _row_rms_scale(y):
    return lax.rsqrt(jnp.mean(y * y, axis=-1, keepdims=True) + EPS)


def _prenorm_kernel(x_ref, g_ref, o_ref):
    x = x_ref[...]
    o_ref[...] = (x * _row_rms_scale(x) * g_ref[...]).astype(o_ref.dtype)


def _prenorm(x2d, gain, tm=512):
    t, d = x2d.shape
    return pl.pallas_call(
        _prenorm_kernel,
        grid=(t // tm,),
        in_specs=[pl.BlockSpec((tm, d), lambda i: (i, 0)),
                  pl.BlockSpec((1, d), lambda i: (0, 0))],
        out_specs=pl.BlockSpec((tm, d), lambda i: (i, 0)),
        out_shape=jax.ShapeDtypeStruct((t, d), BF16),
        compiler_params=_cparams(("parallel",)),
        name="prenorm",
    )(x2d, gain.reshape(1, d))


def _proj_kernel(a_ref, w_ref, *rest, epilogue, n_extra):
    wb_ref = rest[-1]

    @pl.when(pl.program_id(1) == 0)
    def _():
        wb_ref[...] = w_ref[...].astype(wb_ref.dtype)

    acc = jnp.dot(a_ref[...], wb_ref[...], preferred_element_type=F32)
    epilogue(acc, rest[:n_extra], rest[n_extra:-1])


PROJ_TM = 1024
PROJ_TN = 1024


def _proj(a, w, seg, n, extras, extra_specs, out_dtypes, epilogue, name,
          tm=PROJ_TM, tn=PROJ_TN):
    t, k = a.shape
    col0 = seg * (n // tn)
    kern = functools.partial(_proj_kernel, epilogue=epilogue, n_extra=len(extras))
    outs = pl.pallas_call(
        kern,
        grid=(n // tn, t // tm),
        in_specs=[pl.BlockSpec((tm, k), lambda j, i: (i, 0)),
                  pl.BlockSpec((k, tn), lambda j, i: (0, col0 + j))] + list(extra_specs),
        out_specs=[pl.BlockSpec((tm, tn), lambda j, i: (i, j)) for _ in out_dtypes],
        out_shape=[jax.ShapeDtypeStruct((t, n), dt) for dt in out_dtypes],
        scratch_shapes=[pltpu.VMEM((k, tn), a.dtype)],
        compiler_params=_cparams(("parallel", "arbitrary")),
        name=name,
    )(a, w, *extras)
    return outs


def _head_spec():
    return pl.BlockSpec((1, HEAD_DIM), lambda j, i: (0, 0))


def _ep_cast(acc, extras, outs):
    outs[0][...] = acc.astype(outs[0].dtype)


def _ep_silu(acc, extras, outs):
    outs[0][...] = _silu(acc).astype(outs[0].dtype)


def _ep_headnorm(acc, extras, outs, *, scale):
    gain = extras[0][...] * scale
    for h in range(acc.shape[1] // HEAD_DIM):
        cols = slice(h * HEAD_DIM, (h + 1) * HEAD_DIM)
        y = acc[:, cols]
        outs[0][:, cols] = (y * _row_rms_scale(y) * gain).astype(outs[0].dtype)


def _ep_forget(acc, extras, outs, *, layer):
    logits = extras[0][...]
    e = jnp.exp(logits - jnp.max(logits, axis=0, keepdims=True))
    gam = e / jnp.sum(e, axis=0, keepdims=True)
    lb = jnp.sum(gam[:layer + 1], axis=0, keepdims=True) - gam[0:1]
    forget = lb + (1.0 - lb) * _sigmoid(acc)
    outs[0][...] = jnp.log(forget)
    outs[1][...] = (1.0 - forget).astype(outs[1].dtype)


def _fgate_kernel(a_ref, w_ref, b_ref, c_ref, carry_ref, *, tm, n_heads):
    @pl.when(pl.program_id(1) == 0)
    def _():
        carry_ref[...] = jnp.zeros_like(carry_ref)

    f = jnp.dot(a_ref[0], w_ref[...], preferred_element_type=F32) + b_ref[...]
    x = jnp.minimum(f, 0.0) - jnp.log(1.0 + jnp.exp(-jnp.abs(f)))
    rows = lax.broadcasted_iota(jnp.int32, x.shape, 0)
    d = 1
    while d < tm:
        x = x + jnp.where(rows >= d, pltpu.roll(x, d, axis=0), 0.0)
        d *= 2
    x = x + carry_ref[...]
    carry_ref[...] = x[tm - 1:tm, :]
    pieces = jnp.concatenate(_split3(-x * LOG2E), axis=1)
    lanes = x.shape[1]
    rr = lax.broadcasted_iota(jnp.int32, (3 * lanes, lanes), 0)
    cc = lax.broadcasted_iota(jnp.int32, (3 * lanes, lanes), 1)
    head = rr & (lanes - 1)
    piece = lax.shift_right_logical(rr, lanes.bit_length() - 1)
    place = jnp.where((cc == 3 * head + piece) & (head < n_heads), 1.0, 0.0).astype(BF16)
    c_ref[0] = jnp.dot(pieces, place, preferred_element_type=F32).astype(c_ref.dtype)


def _split3(x):
    hi = x.astype(BF16)
    r1 = x - hi.astype(F32)
    mid = r1.astype(BF16)
    lo = (r1 - mid.astype(F32)).astype(BF16)
    return hi, mid, lo


def _fgate(xn3, w_f, b_f, n_heads, tm=512):
    b, s, d = xn3.shape
    lanes = w_f.shape[1]
    assert 3 * n_heads <= lanes
    return pl.pallas_call(
        functools.partial(_fgate_kernel, tm=tm, n_heads=n_heads),
        grid=(b, s // tm),
        in_specs=[pl.BlockSpec((1, tm, d), lambda bi, i: (bi, i, 0)),
                  pl.BlockSpec((d, lanes), lambda bi, i: (0, 0)),
                  pl.BlockSpec((1, lanes), lambda bi, i: (0, 0))],
        out_specs=pl.BlockSpec((1, tm, lanes), lambda bi, i: (bi, i, 0)),
        out_shape=jax.ShapeDtypeStruct((b, s, lanes), BF16),
        scratch_shapes=[pltpu.VMEM((1, lanes), F32)],
        compiler_params=_cparams(("parallel", "arbitrary")),
        name="fox_fgate",
    )(xn3, w_f, b_f)


def _fox_attn_kernel(q_ref, k_ref, v_ref, ck_ref, z_ref, o_ref, acc_ref, s0_ref, s1_ref,
                     kx_ref, *, qb, kb):
    i = pl.program_id(2)
    per_tile = qb // kb
    seq = k_ref.shape[1]
    lane_q = lax.broadcasted_iota(jnp.int32, (qb, HEAD_DIM), 1)
    q = jnp.concatenate(
        [q_ref[0], jnp.where(lane_q < 3, 1.0, 0.0).astype(BF16)], axis=1)

    @pl.when(i == 0)
    def _():
        n_in = ck_ref.shape[2]
        rr = lax.broadcasted_iota(jnp.int32, (n_in, HEAD_DIM), 0)
        cc = lax.broadcasted_iota(jnp.int32, (n_in, HEAD_DIM), 1)
        pick = jnp.where((cc < 3) & (rr == 3 * pl.program_id(1) + cc),
                         1.0, 0.0).astype(BF16)

        for r0 in range(0, seq, qb):
            ext = jnp.dot(ck_ref[0, r0:r0 + qb, :], pick, preferred_element_type=F32)
            kx_ref[r0:r0 + qb, 0:HEAD_DIM] = k_ref[0, r0:r0 + qb, :]
            kx_ref[r0:r0 + qb, HEAD_DIM:2 * HEAD_DIM] = ext.astype(BF16)

    acc_ref[...] = jnp.zeros_like(acc_ref)
    ones = jnp.ones((ONES_ROWS, kb), BF16)

    def scores(t, s_ref):
        for u in range(per_tile):
            r0 = pl.multiple_of(t * qb + u * kb, kb)
            s_ref[u * kb:(u + 1) * kb, :] = lax.dot_general(
                kx_ref[pl.ds(r0, kb), :], q, (((1,), (1,)), ((), ())),
                preferred_element_type=F32)

    def absorb(t, s_ref, m_old, diagonal):
        for u in range(per_tile):
            r0 = pl.multiple_of(t * qb + u * kb, kb)
            vt = jnp.concatenate([v_ref[0, pl.ds(r0, kb), :].T, ones], axis=0)
            c0 = u * kb if diagonal else 0
            s = s_ref[u * kb:(u + 1) * kb, c0:]
            if diagonal:
                tri = (lax.broadcasted_iota(jnp.int32, (kb, kb), 0)
                       <= lax.broadcasted_iota(jnp.int32, (kb, kb), 1))
                parts = [jnp.where(tri, s[:, :kb], NEG)]
                if s.shape[1] > kb:
                    parts.append(s[:, kb:])
                s = jnp.concatenate(parts, axis=1)
            m_part = m_old[:, c0:]
            m_new = jnp.maximum(m_part, jnp.max(s, axis=0, keepdims=True))
            alpha = jnp.exp2(m_part - m_new)
            p = jnp.exp2(s - m_new).astype(BF16)
            acc_ref[:, c0:] = alpha * acc_ref[:, c0:] + jnp.dot(
                vt, p, preferred_element_type=F32)
            m_old = jnp.concatenate([m_old[:, :c0], m_new], axis=1) if c0 else m_new
        return m_old

    def body(a, m):
        t = 2 * a
        scores(t + 1, s1_ref)
        m = absorb(t, s0_ref, m, diagonal=False)
        scores(t + 2, s0_ref)
        return absorb(t + 1, s1_ref, m, diagonal=False)

    scores(0, s0_ref)
    m = lax.fori_loop(0, i // 2, body, jnp.full((1, qb), NEG, F32))
    odd = lax.rem(i, 2) == 1

    @pl.when(jnp.logical_not(odd))
    def _():
        absorb(i, s0_ref, m, diagonal=True)

    @pl.when(odd)
    def _():
        scores(i, s1_ref)
        m1 = absorb(i - 1, s0_ref, m, diagonal=False)
        absorb(i, s1_ref, m1, diagonal=True)

    l = acc_ref[HEAD_DIM:HEAD_DIM + 1, :]
    o = (acc_ref[0:HEAD_DIM, :] * (1.0 / l)).T
    o_ref[0] = (o * z_ref[0].astype(F32)).astype(o_ref.dtype)


def _fox_attn(q3, k3, v3, ck, zs3, n_heads, qb=1024, kb=512):
    b, s, d = q3.shape
    qspec = pl.BlockSpec((1, qb, HEAD_DIM), lambda bi, h, i: (bi, i, h))
    kvspec = pl.BlockSpec((1, s, HEAD_DIM), lambda bi, h, i: (bi, 0, h))
    return pl.pallas_call(
        functools.partial(_fox_attn_kernel, qb=qb, kb=kb),
        grid=(b, n_heads, s // qb),
        in_specs=[qspec, kvspec, kvspec,
                  pl.BlockSpec((1, s, ck.shape[-1]), lambda bi, h, i: (bi, 0, 0)),
                  qspec],
        out_specs=qspec,
        out_shape=jax.ShapeDtypeStruct((b, s, d), BF16),
        scratch_shapes=[pltpu.VMEM((HEAD_DIM + ONES_ROWS, qb), F32),
                        pltpu.VMEM((qb, qb), F32), pltpu.VMEM((qb, qb), F32),
                        pltpu.VMEM((s, 2 * HEAD_DIM), BF16)],
        compiler_params=_cparams(("parallel", "parallel", "arbitrary")),
        name="fox_attn",
    )(q3, k3, v3, ck, zs3)


def _hgrn_kernel(q_ref, k_ref, g_ref, v_ref, z_ref, go_ref, o_ref, st_ref, *, tt, hb):
    c_len = HG_CHUNK

    @pl.when(pl.program_id(2) == 0)
    def _():
        st_ref[...] = jnp.zeros_like(st_ref)

    n_tiles = c_len // SUBLANES
    sub = lax.broadcasted_iota(jnp.int32, (SUBLANES, HEAD_DIM), 0)
    ti = lax.broadcasted_iota(jnp.int32, (c_len, c_len), 0)
    si = lax.broadcasted_iota(jnp.int32, (c_len, c_len), 1)
    txs = ti ^ si
    eye = ti == si
    pair = {n: (ti > si) & (txs >= n) & (txs < 2 * n) for n in HG_LEVELS}
    upper = {n: (sub & n) != 0 for n in HG_LEVELS if n < SUBLANES}
    go = go_ref[...]
    nt_dims = (((1,), (1,)), ((), ()))

    def tiles(x):
        return [x[i * SUBLANES:(i + 1) * SUBLANES, :] for i in range(n_tiles)]

    def decay_products(rows, cols):
        q = q_ref[0, rows, cols].astype(F32)
        k = k_ref[0, rows, cols].astype(F32)
        qt, kt = tiles(q), tiles(k)
        seg = tiles(g_ref[0, rows, cols])
        tot = list(seg)

        diag = jnp.sum(q * k, axis=-1, keepdims=True)
        level_att = []
        for n in HG_LEVELS:
            x = [None] * n_tiles
            if n < SUBLANES:
                up = upper[n]
                for i in range(n_tiles):
                    e = jnp.exp(jnp.where(up, seg[i], tot[i] - seg[i]))
                    x[i] = jnp.where(up, qt[i], kt[i]) * e
                    fwd = pltpu.roll(tot[i], n, axis=0)
                    sib = fwd if 2 * n == SUBLANES else jnp.where(
                        up, fwd, pltpu.roll(tot[i], SUBLANES - n, axis=0))
                    seg[i] = seg[i] + jnp.where(up, sib, 0.0)
                    tot[i] = tot[i] + sib
            else:
                m = n // SUBLANES
                for b0 in range(0, n_tiles, 2 * m):
                    t_lo, t_hi = tot[b0], tot[b0 + m]
                    for i in range(b0, b0 + m):
                        x[i] = kt[i] * jnp.exp(t_lo - seg[i])
                    for i in range(b0 + m, b0 + 2 * m):
                        x[i] = qt[i] * jnp.exp(seg[i])
                        seg[i] = seg[i] + t_lo
                    tot[b0:b0 + 2 * m] = [t_lo + t_hi] * (2 * m)
            xn = jnp.concatenate(x, axis=0).astype(BF16)
            level_att.append(lax.dot_general(xn, xn, nt_dims, preferred_element_type=F32))

        total = tot[0]
        q_in = jnp.concatenate([qt[i] * jnp.exp(seg[i]) for i in range(n_tiles)], axis=0)
        k_out = jnp.concatenate([kt[i] * jnp.exp(total - seg[i]) for i in range(n_tiles)],
                                axis=0)
        return (diag, level_att, q_in.astype(BF16), k_out.astype(BF16),
                jnp.exp(total[0:1, :]))

    def finish(rows, cols, parts, st):
        diag, level_att, q_in, k_out, decay = parts
        v = v_ref[0, rows, cols]
        att = jnp.where(eye, diag, 0.0)
        for n, a_n in zip(HG_LEVELS, level_att):
            att = jnp.where(pair[n], a_n, att)
        intra = jnp.dot(att.astype(BF16), v, preferred_element_type=F32)
        kv = lax.dot_general(v, k_out, (((0,), (0,)), ((), ())),
                             preferred_element_type=F32)
        inter = lax.dot_general(q_in, st.astype(BF16), nt_dims,
                                preferred_element_type=F32)
        y = inter + intra
        y = y * _row_rms_scale(y) * go
        o_ref[0, rows, cols] = (y * z_ref[0, rows, cols].astype(F32)).astype(o_ref.dtype)
        return st * decay + kv

    state = [st_ref[h] for h in range(hb)]
    pending = None
    for ci in range(tt // c_len):
        rows = slice(ci * c_len, (ci + 1) * c_len)
        for h in range(hb):
            cols = slice(h * HEAD_DIM, (h + 1) * HEAD_DIM)
            parts = decay_products(rows, cols)
            if pending is not None:
                p_rows, p_cols, p_h, p_parts = pending
                state[p_h] = finish(p_rows, p_cols, p_parts, state[p_h])
            pending = (rows, cols, h, parts)
    p_rows, p_cols, p_h, p_parts = pending
    state[p_h] = finish(p_rows, p_cols, p_parts, state[p_h])
    for h in range(hb):
        st_ref[h] = state[h]


def _hgrn(q3, k3, g3, v3, zs3, g_o, tt=256, hb=4):
    b, s, d = q3.shape
    n_groups = d // (HEAD_DIM * hb)
    spec = pl.BlockSpec((1, tt, HEAD_DIM * hb), lambda bi, hg, t: (bi, t, hg))
    return pl.pallas_call(
        functools.partial(_hgrn_kernel, tt=tt, hb=hb),
        grid=(b, n_groups, s // tt),
        in_specs=[spec, spec, spec, spec, spec,
                  pl.BlockSpec((1, HEAD_DIM), lambda bi, hg, t: (0, 0))],
        out_specs=spec,
        out_shape=jax.ShapeDtypeStruct((b, s, d), BF16),
        scratch_shapes=[pltpu.VMEM((hb, HEAD_DIM, HEAD_DIM), F32)],
        compiler_params=_cparams(("parallel", "parallel", "arbitrary")),
        name="hgrn2_scan",
    )(q3, k3, g3, v3, zs3, g_o.reshape(1, HEAD_DIM))


def _out_kernel(og_ref, h_ref, p_ref, wo_ref, wpe_ref, wpg_ref, post_ref, *rest, emit_xn):
    y = jnp.dot(og_ref[...], wo_ref[...], preferred_element_type=F32)
    h1 = h_ref[...] + y * _row_rms_scale(y) * post_ref[...]
    pe = jnp.dot(p_ref[...].astype(BF16), wpe_ref[...], preferred_element_type=F32)
    gate = _sigmoid(jnp.dot(h1.astype(BF16), wpg_ref[...], preferred_element_type=F32))
    h2 = h1 + pe * gate
    if emit_xn:
        pre_ref, h_out_ref, xn_ref = rest
        xn_ref[...] = (h2 * _row_rms_scale(h2) * pre_ref[...]).astype(xn_ref.dtype)
    else:
        (h_out_ref,) = rest
    h_out_ref[...] = h2


def _out_block(og, h, p_all, layer, w_out, w_pe, w_pg, post, pre_next, tm=512):
    t, d = h.shape
    pdim = p_all.shape[-1]
    emit_xn = pre_next is not None
    row = lambda i: (i, 0)
    whole = lambda i: (0, 0)
    const = dict(pipeline_mode=pl.Buffered(1))
    in_specs = [pl.BlockSpec((tm, d), row),
                pl.BlockSpec((tm, d), row),
                pl.BlockSpec((None, tm, pdim), lambda i: (layer, i, 0)),
                pl.BlockSpec((d, d), whole, **const),
                pl.BlockSpec((pdim, d), whole, **const),
                pl.BlockSpec((d, d), whole, **const),
                pl.BlockSpec((1, d), whole)]
    args = [og, h, p_all, w_out, w_pe, w_pg, post.reshape(1, d)]
    out_specs = [pl.BlockSpec((tm, d), row)]
    out_shape = [jax.ShapeDtypeStruct((t, d), F32)]
    if emit_xn:
        in_specs.append(pl.BlockSpec((1, d), whole))
        args.append(pre_next.reshape(1, d))
        out_specs.append(pl.BlockSpec((tm, d), row))
        out_shape.append(jax.ShapeDtypeStruct((t, d), BF16))
    outs = pl.pallas_call(
        functools.partial(_out_kernel, emit_xn=emit_xn),
        grid=(t // tm,),
        in_specs=in_specs,
        out_specs=out_specs,
        out_shape=out_shape,
        compiler_params=_cparams(("parallel",)),
        name=f"out_block_{layer}",
    )(*args)
    return outs if emit_xn else (outs[0], None)


def _fox_layer(xn, b, s, w_in, b_f, g_q, g_k):
    t, d = xn.shape
    n_heads = d // HEAD_DIM
    w = w_in
    scale = LOG2E / math.sqrt(HEAD_DIM)
    gq = g_q.reshape(1, HEAD_DIM)
    gk = g_k.reshape(1, HEAD_DIM)
    (q,) = _proj(xn, w, 0, d, [gq], [_head_spec()], [BF16],
                 functools.partial(_ep_headnorm, scale=scale), "fox_proj_q")
    (k,) = _proj(xn, w, 1, d, [gk], [_head_spec()], [BF16],
                 functools.partial(_ep_headnorm, scale=1.0), "fox_proj_k")
    (v,) = _proj(xn, w, 2, d, [], [], [BF16], _ep_cast, "fox_proj_v")
    (zs,) = _proj(xn, w, 3, d, [], [], [BF16], _ep_silu, "fox_proj_z")

    lanes = HEAD_DIM
    w_f = jnp.zeros((d, lanes), BF16).at[:, :n_heads].set(w[:, 4 * d:].astype(BF16))
    b_fp = jnp.zeros((1, lanes), F32).at[0, :n_heads].set(b_f)
    ck = _fgate(xn.reshape(b, s, d), w_f, b_fp, n_heads)

    r3 = lambda a: a.reshape(b, s, d)
    og = _fox_attn(r3(q), r3(k), r3(v), ck, r3(zs), n_heads)
    return og.reshape(t, d)


def _hgrn_layer(xn, b, s, layer, w_in, lb_logits, g_o):
    t, d = xn.shape
    w = w_in
    (q,) = _proj(xn, w, 0, d, [], [], [BF16], _ep_silu, "hg_proj_q")
    depth = lb_logits.shape[0]
    g, k = _proj(xn, w, 1, d, [lb_logits],
                 [pl.BlockSpec((depth, PROJ_TN), lambda j, i: (0, j))], [F32, BF16],
                 functools.partial(_ep_forget, layer=layer), "hg_proj_f")
    (v,) = _proj(xn, w, 2, d, [], [], [BF16], _ep_cast, "hg_proj_i")
    (zs,) = _proj(xn, w, 3, d, [], [], [BF16], _ep_silu, "hg_proj_z")
    r3 = lambda a: a.reshape(b, s, d)
    og = _hgrn(r3(q), r3(k), r3(g), r3(v), r3(zs), g_o)
    return og.reshape(t, d)


def kernel(x, p, w_in_fox, b_f_fox, g_q_fox, g_k_fox, w_out_fox, w_in_hg, lb_logits,
           g_o_hg, w_out_hg, pre_norm, post_norm, w_pe, w_pg):
    b, s, d = x.shape
    depth = p.shape[0]
    t = b * s
    h = x.reshape(t, d)
    p_all = p.reshape(depth, t, p.shape[-1])
    xn = _prenorm(h, pre_norm[0])
    for i in range(depth):
        j = i // 2
        if i % 2 == 0:
            og = _fox_layer(xn, b, s, w_in_fox[j], b_f_fox[j], g_q_fox[j], g_k_fox[j])
            w_out = w_out_fox[j]
        else:
            og = _hgrn_layer(xn, b, s, i, w_in_hg[j], lb_logits, g_o_hg[j])
            w_out = w_out_hg[j]
        pre_next = pre_norm[i + 1] if i + 1 < depth else None
        h, xn = _out_block(og, h, p_all, i, w_out.astype(BF16), w_pe[i].astype(BF16),
                           w_pg[i].astype(BF16), post_norm[i], pre_next)
    return h.reshape(b, s, d)
```

```python
import functools
import math

import jax
import jax.numpy as jnp
from jax import lax
from jax.experimental import pallas as pl
from jax.experimental.pallas import tpu as pltpu

HEAD_DIM = 128
EPS = 1e-6
NEG = -1e30
LOG2E = math.log2(math.e)
ONES_ROWS = 16
HG_CHUNK = 64
HG_LEVELS = (1, 2, 4, 8, 16, 32)
SUBLANES = 8

F32 = jnp.float32
BF16 = jnp.bfloat16

VMEM_LIMIT_BYTES = 56 * 1024 * 1024


def _cparams(semantics):
    return pltpu.CompilerParams(dimension_semantics=semantics,
                                vmem_limit_bytes=VMEM_LIMIT_BYTES)


def _sigmoid(x):
    return 1.0 / (1.0 + jnp.exp(-x))


def _silu(x):
    return x * _sigmoid(x)


def _row_rms_scale(y):
    return lax.rsqrt(jnp.mean(y * y, axis=-1, keepdims=True) + EPS)


def _prenorm_kernel(x_ref, g_ref, o_ref):
    x = x_ref[...]
    o_ref[...] = (x * _row_rms_scale(x) * g_ref[...]).astype(o_ref.dtype)


def _prenorm(x2d, gain, tm=512):
    t, d = x2d.shape
    return pl.pallas_call(
        _prenorm_kernel,
        grid=(t // tm,),
        in_specs=[pl.BlockSpec((tm, d), lambda i: (i, 0)),
                  pl.BlockSpec((1, d), lambda i: (0, 0))],
        out_specs=pl.BlockSpec((tm, d), lambda i: (i, 0)),
        out_shape=jax.ShapeDtypeStruct((t, d), BF16),
        compiler_params=_cparams(("parallel",)),
        name="prenorm",
    )(x2d, gain.reshape(1, d))


def _proj_kernel(a_ref, w_ref, *rest, epilogue, n_extra):
    wb_ref = rest[-1]

    @pl.when(pl.program_id(1) == 0)
    def _():
        wb_ref[...] = w_ref[...].astype(wb_ref.dtype)

    rows = a_ref.shape[0] // PROJ_ROW_PARTS
    parts = [slice(i * rows, (i + 1) * rows) for i in range(PROJ_ROW_PARTS)]
    accs = [jnp.dot(a_ref[r, :], wb_ref[...], preferred_element_type=F32) for r in parts]
    for r, acc in zip(parts, accs):
        epilogue(acc, rest[:n_extra], rest[n_extra:-1], r)


PROJ_ROW_PARTS = 4


PROJ_TM = 1024
PROJ_TN = 1024


def _proj(a, w, seg, n, extras, extra_specs, out_dtypes, epilogue, name,
          tm=PROJ_TM, tn=PROJ_TN):
    t, k = a.shape
    col0 = seg * (n // tn)
    kern = functools.partial(_proj_kernel, epilogue=epilogue, n_extra=len(extras))
    outs = pl.pallas_call(
        kern,
        grid=(n // tn, t // tm),
        in_specs=[pl.BlockSpec((tm, k), lambda j, i: (i, 0)),
                  pl.BlockSpec((k, tn), lambda j, i: (0, col0 + j))] + list(extra_specs),
        out_specs=[pl.BlockSpec((tm, tn), lambda j, i: (i, j)) for _ in out_dtypes],
        out_shape=[jax.ShapeDtypeStruct((t, n), dt) for dt in out_dtypes],
        scratch_shapes=[pltpu.VMEM((k, tn), a.dtype)],
        compiler_params=_cparams(("parallel", "arbitrary")),
        name=name,
    )(a, w, *extras)
    return outs


def _head_spec():
    return pl.BlockSpec((1, HEAD_DIM), lambda j, i: (0, 0))


def _ep_cast(acc, extras, outs, rows):
    outs[0][rows, :] = acc.astype(outs[0].dtype)


def _ep_silu(acc, extras, outs, rows):
    outs[0][rows, :] = _silu(acc).astype(outs[0].dtype)


def _ep_headnorm(acc, extras, outs, rows, *, scale):
    gain = extras[0][...] * scale
    for h in range(acc.shape[1] // HEAD_DIM):
        cols = slice(h * HEAD_DIM, (h + 1) * HEAD_DIM)
        y = acc[:, cols]
        outs[0][rows, cols] = (y * _row_rms_scale(y) * gain).astype(outs[0].dtype)


def _ep_forget(acc, extras, outs, rows, *, layer):
    logits = extras[0][...]
    e = jnp.exp(logits - jnp.max(logits, axis=0, keepdims=True))
    gam = e / jnp.sum(e, axis=0, keepdims=True)
    lb = jnp.sum(gam[:layer + 1], axis=0, keepdims=True) - gam[0:1]
    forget = lb + (1.0 - lb) * _sigmoid(acc)
    outs[0][rows, :] = jnp.log(forget)
    outs[1][rows, :] = (1.0 - forget).astype(outs[1].dtype)


def _fgate_kernel(a_ref, w_ref, b_ref, c_ref, carry_ref, *, tm, n_heads):
    @pl.when(pl.program_id(1) == 0)
    def _():
        carry_ref[...] = jnp.zeros_like(carry_ref)

    f = jnp.dot(a_ref[0], w_ref[...], preferred_element_type=F32) + b_ref[...]
    x = jnp.minimum(f, 0.0) - jnp.log(1.0 + jnp.exp(-jnp.abs(f)))
    rows = lax.broadcasted_iota(jnp.int32, x.shape, 0)
    d = 1
    while d < tm:
        x = x + jnp.where(rows >= d, pltpu.roll(x, d, axis=0), 0.0)
        d *= 2
    x = x + carry_ref[...]
    carry_ref[...] = x[tm - 1:tm, :]
    pieces = jnp.concatenate(_split3(-x * LOG2E), axis=1)
    lanes = x.shape[1]
    rr = lax.broadcasted_iota(jnp.int32, (3 * lanes, lanes), 0)
    cc = lax.broadcasted_iota(jnp.int32, (3 * lanes, lanes), 1)
    head = rr & (lanes - 1)
    piece = lax.shift_right_logical(rr, lanes.bit_length() - 1)
    place = jnp.where((cc == 3 * head + piece) & (head < n_heads), 1.0, 0.0).astype(BF16)
    c_ref[0] = jnp.dot(pieces, place, preferred_element_type=F32).astype(c_ref.dtype)


def _split3(x):
    hi = x.astype(BF16)
    r1 = x - hi.astype(F32)
    mid = r1.astype(BF16)
    lo = (r1 - mid.astype(F32)).astype(BF16)
    return hi, mid, lo


def _fgate(xn3, w_f, b_f, n_heads, tm=512):
    b, s, d = xn3.shape
    lanes = w_f.shape[1]
    assert 3 * n_heads <= lanes
    return pl.pallas_call(
        functools.partial(_fgate_kernel, tm=tm, n_heads=n_heads),
        grid=(b, s // tm),
        in_specs=[pl.BlockSpec((1, tm, d), lambda bi, i: (bi, i, 0)),
                  pl.BlockSpec((d, lanes), lambda bi, i: (0, 0)),
                  pl.BlockSpec((1, lanes), lambda bi, i: (0, 0))],
        out_specs=pl.BlockSpec((1, tm, lanes), lambda bi, i: (bi, i, 0)),
        out_shape=jax.ShapeDtypeStruct((b, s, lanes), BF16),
        scratch_shapes=[pltpu.VMEM((1, lanes), F32)],
        compiler_params=_cparams(("parallel", "arbitrary")),
        name="fox_fgate",
    )(xn3, w_f, b_f)


def _fox_attn_kernel(q_ref, k_ref, v_ref, ck_ref, z_ref, o_ref, acc_ref, s0_ref, s1_ref,
                     kx_ref, *, qb, kb):
    i = pl.program_id(2)
    per_tile = qb // kb
    seq = k_ref.shape[1]
    lane_q = lax.broadcasted_iota(jnp.int32, (qb, HEAD_DIM), 1)
    q = jnp.concatenate(
        [q_ref[0], jnp.where(lane_q < 3, 1.0, 0.0).astype(BF16)], axis=1)

    @pl.when(i == 0)
    def _():
        n_in = ck_ref.shape[2]
        rr = lax.broadcasted_iota(jnp.int32, (n_in, HEAD_DIM), 0)
        cc = lax.broadcasted_iota(jnp.int32, (n_in, HEAD_DIM), 1)
        pick = jnp.where((cc < 3) & (rr == 3 * pl.program_id(1) + cc),
                         1.0, 0.0).astype(BF16)

        for r0 in range(0, seq, qb):
            ext = jnp.dot(ck_ref[0, r0:r0 + qb, :], pick, preferred_element_type=F32)
            kx_ref[r0:r0 + qb, 0:HEAD_DIM] = k_ref[0, r0:r0 + qb, :]
            kx_ref[r0:r0 + qb, HEAD_DIM:2 * HEAD_DIM] = ext.astype(BF16)

    acc_ref[...] = jnp.zeros_like(acc_ref)
    ones = jnp.ones((ONES_ROWS, kb), BF16)

    def scores(t, s_ref):
        for u in range(per_tile):
            r0 = pl.multiple_of(t * qb + u * kb, kb)
            s_ref[u * kb:(u + 1) * kb, :] = lax.dot_general(
                kx_ref[pl.ds(r0, kb), :], q, (((1,), (1,)), ((), ())),
                preferred_element_type=F32)

    def absorb(t, s_ref, m_old, diagonal):
        for u in range(per_tile):
            r0 = pl.multiple_of(t * qb + u * kb, kb)
            vt = jnp.concatenate([v_ref[0, pl.ds(r0, kb), :].T, ones], axis=0)
            c0 = u * kb if diagonal else 0
            s = s_ref[u * kb:(u + 1) * kb, c0:]
            if diagonal:
                tri = (lax.broadcasted_iota(jnp.int32, (kb, kb), 0)
                       <= lax.broadcasted_iota(jnp.int32, (kb, kb), 1))
                parts = [jnp.where(tri, s[:, :kb], NEG)]
                if s.shape[1] > kb:
                    parts.append(s[:, kb:])
                s = jnp.concatenate(parts, axis=1)
            m_part = m_old[:, c0:]
            m_new = jnp.maximum(m_part, jnp.max(s, axis=0, keepdims=True))
            alpha = jnp.exp2(m_part - m_new)
            p = jnp.exp2(s - m_new).astype(BF16)
            acc_ref[:, c0:] = alpha * acc_ref[:, c0:] + jnp.dot(
                vt, p, preferred_element_type=F32)
            m_old = jnp.concatenate([m_old[:, :c0], m_new], axis=1) if c0 else m_new
        return m_old

    def body(a, m):
        t = 2 * a
        scores(t + 1, s1_ref)
        m = absorb(t, s0_ref, m, diagonal=False)
        scores(t + 2, s0_ref)
        return absorb(t + 1, s1_ref, m, diagonal=False)

    scores(0, s0_ref)
    m = lax.fori_loop(0, i // 2, body, jnp.full((1, qb), NEG, F32))
    odd = lax.rem(i, 2) == 1

    @pl.when(jnp.logical_not(odd))
    def _():
        absorb(i, s0_ref, m, diagonal=True)

    @pl.when(odd)
    def _():
        scores(i, s1_ref)
        m1 = absorb(i - 1, s0_ref, m, diagonal=False)
        absorb(i, s1_ref, m1, diagonal=True)

    l = acc_ref[HEAD_DIM:HEAD_DIM + 1, :]
    o = (acc_ref[0:HEAD_DIM, :] * (1.0 / l)).T
    o_ref[0] = (o * z_ref[0].astype(F32)).astype(o_ref.dtype)


def _fox_attn(q3, k3, v3, ck, zs3, n_heads, qb=1024, kb=512):
    b, s, d = q3.shape
    qspec = pl.BlockSpec((1, qb, HEAD_DIM), lambda bi, h, i: (bi, i, h))
    kvspec = pl.BlockSpec((1, s, HEAD_DIM), lambda bi, h, i: (bi, 0, h))
    return pl.pallas_call(
        functools.partial(_fox_attn_kernel, qb=qb, kb=kb),
        grid=(b, n_heads, s // qb),
        in_specs=[qspec, kvspec, kvspec,
                  pl.BlockSpec((1, s, ck.shape[-1]), lambda bi, h, i: (bi, 0, 0)),
                  qspec],
        out_specs=qspec,
        out_shape=jax.ShapeDtypeStruct((b, s, d), BF16),
        scratch_shapes=[pltpu.VMEM((HEAD_DIM + ONES_ROWS, qb), F32),
                        pltpu.VMEM((qb, qb), F32), pltpu.VMEM((qb, qb), F32),
                        pltpu.VMEM((s, 2 * HEAD_DIM), BF16)],
        compiler_params=_cparams(("parallel", "parallel", "arbitrary")),
        name="fox_attn",
    )(q3, k3, v3, ck, zs3)


def _hgrn_kernel(q_ref, k_ref, g_ref, v_ref, z_ref, go_ref, o_ref, st_ref, *, tt, hb):
    c_len = HG_CHUNK

    @pl.when(pl.program_id(2) == 0)
    def _():
        st_ref[...] = jnp.zeros_like(st_ref)

    n_tiles = c_len // SUBLANES
    sub = lax.broadcasted_iota(jnp.int32, (SUBLANES, HEAD_DIM), 0)
    ti = lax.broadcasted_iota(jnp.int32, (c_len, c_len), 0)
    si = lax.broadcasted_iota(jnp.int32, (c_len, c_len), 1)
    txs = ti ^ si
    eye = ti == si
    pair = {n: (ti > si) & (txs >= n) & (txs < 2 * n) for n in HG_LEVELS}
    upper = {n: (sub & n) != 0 for n in HG_LEVELS if n < SUBLANES}
    go = go_ref[...]
    nt_dims = (((1,), (1,)), ((), ()))

    def tiles(x):
        return [x[i * SUBLANES:(i + 1) * SUBLANES, :] for i in range(n_tiles)]

    def decay_products(rows, cols):
        q = q_ref[0, rows, cols].astype(F32)
        k = k_ref[0, rows, cols].astype(F32)
        qt, kt = tiles(q), tiles(k)
        seg = tiles(g_ref[0, rows, cols])
        tot = list(seg)

        diag = jnp.sum(q * k, axis=-1, keepdims=True)
        level_att = []
        for n in HG_LEVELS:
            x = [None] * n_tiles
            if n < SUBLANES:
                up = upper[n]
                for i in range(n_tiles):
                    e = jnp.exp(jnp.where(up, seg[i], tot[i] - seg[i]))
                    x[i] = jnp.where(up, qt[i], kt[i]) * e
                    fwd = pltpu.roll(tot[i], n, axis=0)
                    sib = fwd if 2 * n == SUBLANES else jnp.where(
                        up, fwd, pltpu.roll(tot[i], SUBLANES - n, axis=0))
                    seg[i] = seg[i] + jnp.where(up, sib, 0.0)
                    tot[i] = tot[i] + sib
            else:
                m = n // SUBLANES
                for b0 in range(0, n_tiles, 2 * m):
                    t_lo, t_hi = tot[b0], tot[b0 + m]
                    for i in range(b0, b0 + m):
                        x[i] = kt[i] * jnp.exp(t_lo - seg[i])
                    for i in range(b0 + m, b0 + 2 * m):
                        x[i] = qt[i] * jnp.exp(seg[i])
                        seg[i] = seg[i] + t_lo
                    tot[b0:b0 + 2 * m] = [t_lo + t_hi] * (2 * m)
            xn = jnp.concatenate(x, axis=0).astype(BF16)
            level_att.append(lax.dot_general(xn, xn, nt_dims, preferred_element_type=F32))

        total = tot[0]
        q_in = jnp.concatenate([qt[i] * jnp.exp(seg[i]) for i in range(n_tiles)], axis=0)
        k_out = jnp.concatenate([kt[i] * jnp.exp(total - seg[i]) for i in range(n_tiles)],
                                axis=0)
        return (diag, level_att, q_in.astype(BF16), k_out.astype(BF16),
                jnp.exp(total[0:1, :]))

    def finish(rows, cols, parts, st):
        diag, level_att, q_in, k_out, decay = parts
        v = v_ref[0, rows, cols]
        att = jnp.where(eye, diag, 0.0)
        for n, a_n in zip(HG_LEVELS, level_att):
            att = jnp.where(pair[n], a_n, att)
        intra = jnp.dot(att.astype(BF16), v, preferred_element_type=F32)
        kv = lax.dot_general(v, k_out, (((0,), (0,)), ((), ())),
                             preferred_element_type=F32)
        inter = lax.dot_general(q_in, st.astype(BF16), nt_dims,
                                preferred_element_type=F32)
        y = inter + intra
        y = y * _row_rms_scale(y) * go
        o_ref[0, rows, cols] = (y * z_ref[0, rows, cols].astype(F32)).astype(o_ref.dtype)
        return st * decay + kv

    state = [st_ref[h] for h in range(hb)]
    pending = None
    for ci in range(tt // c_len):
        rows = slice(ci * c_len, (ci + 1) * c_len)
        for h in range(hb):
            cols = slice(h * HEAD_DIM, (h + 1) * HEAD_DIM)
            parts = decay_products(rows, cols)
            if pending is not None:
                p_rows, p_cols, p_h, p_parts = pending
                state[p_h] = finish(p_rows, p_cols, p_parts, state[p_h])
            pending = (rows, cols, h, parts)
    p_rows, p_cols, p_h, p_parts = pending
    state[p_h] = finish(p_rows, p_cols, p_parts, state[p_h])
    for h in range(hb):
        st_ref[h] = state[h]


def _hgrn(q3, k3, g3, v3, zs3, g_o, tt=256, hb=8):
    b, s, d = q3.shape
    n_groups = d // (HEAD_DIM * hb)
    spec = pl.BlockSpec((1, tt, HEAD_DIM * hb), lambda bi, hg, t: (bi, t, hg))
    return pl.pallas_call(
        functools.partial(_hgrn_kernel, tt=tt, hb=hb),
        grid=(b, n_groups, s // tt),
        in_specs=[spec, spec, spec, spec, spec,
                  pl.BlockSpec((1, HEAD_DIM), lambda bi, hg, t: (0, 0))],
        out_specs=spec,
        out_shape=jax.ShapeDtypeStruct((b, s, d), BF16),
        scratch_shapes=[pltpu.VMEM((hb, HEAD_DIM, HEAD_DIM), F32)],
        compiler_params=_cparams(("parallel", "parallel", "arbitrary")),
        name="hgrn2_scan",
    )(q3, k3, g3, v3, zs3, g_o.reshape(1, HEAD_DIM))


def _out_kernel(og_ref, h_ref, p_ref, wo_ref, wpe_ref, wpg_ref, post_ref, *rest, emit_xn):
    if emit_xn:
        pre_ref, h_out_ref, xn_ref = rest
    else:
        (h_out_ref,) = rest
    y = jnp.dot(og_ref[...], wo_ref[...], preferred_element_type=F32)
    h1 = h_ref[...] + y * _row_rms_scale(y) * post_ref[...]
    pe = jnp.dot(p_ref[...].astype(BF16), wpe_ref[...], preferred_element_type=F32)
    gate = _sigmoid(jnp.dot(h1.astype(BF16), wpg_ref[...], preferred_element_type=F32))
    h2 = h1 + pe * gate
    if emit_xn:
        xn_ref[...] = (h2 * _row_rms_scale(h2) * pre_ref[...]).astype(xn_ref.dtype)
    h_out_ref[...] = h2


def _out_block(og, h, p_all, layer, w_out, w_pe, w_pg, post, pre_next, tm=512):
    t, d = h.shape
    pdim = p_all.shape[-1]
    emit_xn = pre_next is not None
    row = lambda i: (i, 0)
    whole = lambda i: (0, 0)
    const = dict(pipeline_mode=pl.Buffered(1))
    in_specs = [pl.BlockSpec((tm, d), row),
                pl.BlockSpec((tm, d), row),
                pl.BlockSpec((None, tm, pdim), lambda i: (layer, i, 0)),
                pl.BlockSpec((d, d), whole, **const),
                pl.BlockSpec((pdim, d), whole, **const),
                pl.BlockSpec((d, d), whole, **const),
                pl.BlockSpec((1, d), whole)]
    args = [og, h, p_all, w_out, w_pe, w_pg, post.reshape(1, d)]
    out_specs = [pl.BlockSpec((tm, d), row)]
    out_shape = [jax.ShapeDtypeStruct((t, d), F32)]
    if emit_xn:
        in_specs.append(pl.BlockSpec((1, d), whole))
        args.append(pre_next.reshape(1, d))
        out_specs.append(pl.BlockSpec((tm, d), row))
        out_shape.append(jax.ShapeDtypeStruct((t, d), BF16))
    outs = pl.pallas_call(
        functools.partial(_out_kernel, emit_xn=emit_xn),
        grid=(t // tm,),
        in_specs=in_specs,
        out_specs=out_specs,
        out_shape=out_shape,
        compiler_params=_cparams(("parallel",)),
        name=f"out_block_{layer}",
    )(*args)
    return outs if emit_xn else (outs[0], None)


def _fox_layer(xn, b, s, w_in, b_f, g_q, g_k):
    t, d = xn.shape
    n_heads = d // HEAD_DIM
    w = w_in
    scale = LOG2E / math.sqrt(HEAD_DIM)
    gq = g_q.reshape(1, HEAD_DIM)
    gk = g_k.reshape(1, HEAD_DIM)
    (q,) = _proj(xn, w, 0, d, [gq], [_head_spec()], [BF16],
                 functools.partial(_ep_headnorm, scale=scale), "fox_proj_q")
    (k,) = _proj(xn, w, 1, d, [gk], [_head_spec()], [BF16],
                 functools.partial(_ep_headnorm, scale=1.0), "fox_proj_k")
    (v,) = _proj(xn, w, 2, d, [], [], [BF16], _ep_cast, "fox_proj_v")
    (zs,) = _proj(xn, w, 3, d, [], [], [BF16], _ep_silu, "fox_proj_z")

    lanes = HEAD_DIM
    w_f = jnp.zeros((d, lanes), BF16).at[:, :n_heads].set(w[:, 4 * d:].astype(BF16))
    b_fp = jnp.zeros((1, lanes), F32).at[0, :n_heads].set(b_f)
    ck = _fgate(xn.reshape(b, s, d), w_f, b_fp, n_heads)

    r3 = lambda a: a.reshape(b, s, d)
    og = _fox_attn(r3(q), r3(k), r3(v), ck, r3(zs), n_heads)
    return og.reshape(t, d)


def _hgrn_layer(xn, b, s, layer, w_in, lb_logits, g_o):
    t, d = xn.shape
    w = w_in
    (q,) = _proj(xn, w, 0, d, [], [], [BF16], _ep_silu, "hg_proj_q")
    depth = lb_logits.shape[0]
    g, k = _proj(xn, w, 1, d, [lb_logits],
                 [pl.BlockSpec((depth, PROJ_TN), lambda j, i: (0, j))], [F32, BF16],
                 functools.partial(_ep_forget, layer=layer), "hg_proj_f")
    (v,) = _proj(xn, w, 2, d, [], [], [BF16], _ep_cast, "hg_proj_i")
    (zs,) = _proj(xn, w, 3, d, [], [], [BF16], _ep_silu, "hg_proj_z")
    r3 = lambda a: a.reshape(b, s, d)
    og = _hgrn(r3(q), r3(k), r3(g), r3(v), r3(zs), g_o)
    return og.reshape(t, d)


def kernel(x, p, w_in_fox, b_f_fox, g_q_fox, g_k_fox, w_out_fox, w_in_hg, lb_logits,
           g_o_hg, w_out_hg, pre_norm, post_norm, w_pe, w_pg):
    b, s, d = x.shape
    depth = p.shape[0]
    t = b * s
    h = x.reshape(t, d)
    p_all = p.reshape(depth, t, p.shape[-1])
    xn = _prenorm(h, pre_norm[0])
    for i in range(depth):
        j = i // 2
        if i % 2 == 0:
            og = _fox_layer(xn, b, s, w_in_fox[j], b_f_fox[j], g_q_fox[j], g_k_fox[j])
            w_out = w_out_fox[j]
        else:
            og = _hgrn_layer(xn, b, s, i, w_in_hg[j], lb_logits, g_o_hg[j])
            w_out = w_out_hg[j]
        pre_next = pre_norm[i + 1] if i + 1 < depth else None
        h, xn = _out_block(og, h, p_all, i, w_out.astype(BF16), w_pe[i].astype(BF16),
                           w_pg[i].astype(BF16), post_norm[i], pre_next)
    return h.reshape(b, s, d)
```

```python
import functools
import math

import jax
import jax.numpy as jnp
from jax import lax
from jax.experimental import pallas as pl
from jax.experimental.pallas import tpu as pltpu

HEAD_DIM = 128
EPS = 1e-6
NEG = -1e30
LOG2E = math.log2(math.e)
ONES_ROWS = 16
HG_CHUNK = 64
HG_LEVELS = (1, 2, 4, 8, 16, 32)
SUBLANES = 8

F32 = jnp.float32
BF16 = jnp.bfloat16

VMEM_LIMIT_BYTES = 56 * 1024 * 1024


def _cparams(semantics):
    return pltpu.CompilerParams(dimension_semantics=semantics,
                                vmem_limit_bytes=VMEM_LIMIT_BYTES)


def _sigmoid(x):
    return 1.0 / (1.0 + jnp.exp(-x))


def _silu(x):
    return x * _sigmoid(x)


def _row_rms_scale(y):
    return lax.rsqrt(jnp.mean(y * y, axis=-1, keepdims=True) + EPS)


def _prenorm_kernel(x_ref, g_ref, o_ref):
    x = x_ref[...]
    o_ref[...] = (x * _row_rms_scale(x) * g_ref[...]).astype(o_ref.dtype)


def _prenorm(x2d, gain, tm=512):
    t, d = x2d.shape
    return pl.pallas_call(
        _prenorm_kernel,
        grid=(t // tm,),
        in_specs=[pl.BlockSpec((tm, d), lambda i: (i, 0)),
                  pl.BlockSpec((1, d), lambda i: (0, 0))],
        out_specs=pl.BlockSpec((tm, d), lambda i: (i, 0)),
        out_shape=jax.ShapeDtypeStruct((t, d), BF16),
        compiler_params=_cparams(("parallel",)),
        name="prenorm",
    )(x2d, gain.reshape(1, d))


def _proj_kernel(a_ref, w_ref, *rest, epilogue, n_extra, row_parts):
    wb_ref = rest[-1]

    @pl.when(pl.program_id(1) == 0)
    def _():
        wb_ref[...] = w_ref[...].astype(wb_ref.dtype)

    rows = a_ref.shape[0] // row_parts
    parts = [slice(i * rows, (i + 1) * rows) for i in range(row_parts)]
    accs = [jnp.dot(a_ref[r, :], wb_ref[...], preferred_element_type=F32) for r in parts]
    for r, acc in zip(parts, accs):
        epilogue(acc, rest[:n_extra], rest[n_extra:-1], r)


PROJ_ROW_PARTS = 4
PROJ_TM = 1024
PROJ_TN = 1024


def _proj(a, w, seg, n, extras, extra_specs, out_dtypes, epilogue, name,
          tm=PROJ_TM, tn=PROJ_TN):
    t, k = a.shape
    col0 = seg * (n // tn)
    kern = functools.partial(_proj_kernel, epilogue=epilogue, n_extra=len(extras),
                             row_parts=1 if epilogue is _ep_cast else PROJ_ROW_PARTS)
    outs = pl.pallas_call(
        kern,
        grid=(n // tn, t // tm),
        in_specs=[pl.BlockSpec((tm, k), lambda j, i: (i, 0)),
                  pl.BlockSpec((k, tn), lambda j, i: (0, col0 + j))] + list(extra_specs),
        out_specs=[pl.BlockSpec((tm, tn), lambda j, i: (i, j)) for _ in out_dtypes],
        out_shape=[jax.ShapeDtypeStruct((t, n), dt) for dt in out_dtypes],
        scratch_shapes=[pltpu.VMEM((k, tn), a.dtype)],
        compiler_params=_cparams(("parallel", "arbitrary")),
        name=name,
    )(a, w, *extras)
    return outs


def _head_spec():
    return pl.BlockSpec((1, HEAD_DIM), lambda j, i: (0, 0))


def _ep_cast(acc, extras, outs, rows):
    outs[0][rows, :] = acc.astype(outs[0].dtype)


def _ep_silu(acc, extras, outs, rows):
    outs[0][rows, :] = _silu(acc).astype(outs[0].dtype)


def _ep_headnorm(acc, extras, outs, rows, *, scale):
    gain = extras[0][...] * scale
    for h in range(acc.shape[1] // HEAD_DIM):
        cols = slice(h * HEAD_DIM, (h + 1) * HEAD_DIM)
        y = acc[:, cols]
        outs[0][rows, cols] = (y * _row_rms_scale(y) * gain).astype(outs[0].dtype)


def _ep_forget(acc, extras, outs, rows, *, layer):
    logits = extras[0][...]
    e = jnp.exp(logits - jnp.max(logits, axis=0, keepdims=True))
    gam = e / jnp.sum(e, axis=0, keepdims=True)
    lb = jnp.sum(gam[:layer + 1], axis=0, keepdims=True) - gam[0:1]
    forget = lb + (1.0 - lb) * _sigmoid(acc)
    outs[0][rows, :] = jnp.log(forget)
    outs[1][rows, :] = (1.0 - forget).astype(outs[1].dtype)


def _fgate_kernel(a_ref, w_ref, b_ref, c_ref, carry_ref, *, tm, n_heads):
    @pl.when(pl.program_id(1) == 0)
    def _():
        carry_ref[...] = jnp.zeros_like(carry_ref)

    f = jnp.dot(a_ref[0], w_ref[...], preferred_element_type=F32) + b_ref[...]
    x = jnp.minimum(f, 0.0) - jnp.log(1.0 + jnp.exp(-jnp.abs(f)))
    rows = lax.broadcasted_iota(jnp.int32, x.shape, 0)
    d = 1
    while d < tm:
        x = x + jnp.where(rows >= d, pltpu.roll(x, d, axis=0), 0.0)
        d *= 2
    x = x + carry_ref[...]
    carry_ref[...] = x[tm - 1:tm, :]
    pieces = jnp.concatenate(_split3(-x * LOG2E), axis=1)
    lanes = x.shape[1]
    rr = lax.broadcasted_iota(jnp.int32, (3 * lanes, lanes), 0)
    cc = lax.broadcasted_iota(jnp.int32, (3 * lanes, lanes), 1)
    head = rr & (lanes - 1)
    piece = lax.shift_right_logical(rr, lanes.bit_length() - 1)
    place = jnp.where((cc == 3 * head + piece) & (head < n_heads), 1.0, 0.0).astype(BF16)
    c_ref[0] = jnp.dot(pieces, place, preferred_element_type=F32).astype(c_ref.dtype)


def _split3(x):
    hi = x.astype(BF16)
    r1 = x - hi.astype(F32)
    mid = r1.astype(BF16)
    lo = (r1 - mid.astype(F32)).astype(BF16)
    return hi, mid, lo


def _fgate(xn3, w_f, b_f, n_heads, tm=512):
    b, s, d = xn3.shape
    lanes = w_f.shape[1]
    assert 3 * n_heads <= lanes
    return pl.pallas_call(
        functools.partial(_fgate_kernel, tm=tm, n_heads=n_heads),
        grid=(b, s // tm),
        in_specs=[pl.BlockSpec((1, tm, d), lambda bi, i: (bi, i, 0)),
                  pl.BlockSpec((d, lanes), lambda bi, i: (0, 0)),
                  pl.BlockSpec((1, lanes), lambda bi, i: (0, 0))],
        out_specs=pl.BlockSpec((1, tm, lanes), lambda bi, i: (bi, i, 0)),
        out_shape=jax.ShapeDtypeStruct((b, s, lanes), BF16),
        scratch_shapes=[pltpu.VMEM((1, lanes), F32)],
        compiler_params=_cparams(("parallel", "arbitrary")),
        name="fox_fgate",
    )(xn3, w_f, b_f)


def _fox_attn_kernel(q_ref, k_ref, v_ref, ck_ref, z_ref, o_ref, acc_ref, s0_ref, s1_ref,
                     kx_ref, vt_ref, *, qb, kb):
    i = pl.program_id(2)
    per_tile = qb // kb
    seq = k_ref.shape[1]
    lane_q = lax.broadcasted_iota(jnp.int32, (qb, HEAD_DIM), 1)
    q = jnp.concatenate(
        [q_ref[0], jnp.where(lane_q < 3, 1.0, 0.0).astype(BF16)], axis=1)

    @pl.when(i == 0)
    def _():
        n_in = ck_ref.shape[2]
        rr = lax.broadcasted_iota(jnp.int32, (n_in, HEAD_DIM), 0)
        cc = lax.broadcasted_iota(jnp.int32, (n_in, HEAD_DIM), 1)
        pick = jnp.where((cc < 3) & (rr == 3 * pl.program_id(1) + cc),
                         1.0, 0.0).astype(BF16)

        for r0 in range(0, seq, qb):
            ext = jnp.dot(ck_ref[0, r0:r0 + qb, :], pick, preferred_element_type=F32)
            kx_ref[r0:r0 + qb, 0:HEAD_DIM] = k_ref[0, r0:r0 + qb, :]
            kx_ref[r0:r0 + qb, HEAD_DIM:2 * HEAD_DIM] = ext.astype(BF16)
        ones = jnp.ones((ONES_ROWS, kb), BF16)
        for j in range(seq // kb):
            vt_ref[j] = jnp.concatenate([v_ref[0, j * kb:(j + 1) * kb, :].T, ones], axis=0)

    acc_ref[...] = jnp.zeros_like(acc_ref)

    def scores(t, s_ref):
        for u in range(per_tile):
            r0 = pl.multiple_of(t * qb + u * kb, kb)
            s_ref[u * kb:(u + 1) * kb, :] = lax.dot_general(
                kx_ref[pl.ds(r0, kb), :], q, (((1,), (1,)), ((), ())),
                preferred_element_type=F32)

    def absorb(t, s_ref, m_old, diagonal):
        for u in range(per_tile):
            vt = vt_ref[t * per_tile + u]
            c0 = u * kb if diagonal else 0
            s = s_ref[u * kb:(u + 1) * kb, c0:]
            if diagonal:
                tri = (lax.broadcasted_iota(jnp.int32, (kb, kb), 0)
                       <= lax.broadcasted_iota(jnp.int32, (kb, kb), 1))
                parts = [jnp.where(tri, s[:, :kb], NEG)]
                if s.shape[1] > kb:
                    parts.append(s[:, kb:])
                s = jnp.concatenate(parts, axis=1)
            m_part = m_old[:, c0:]
            m_new = jnp.maximum(m_part, jnp.max(s, axis=0, keepdims=True))
            alpha = jnp.exp2(m_part - m_new)
            p = jnp.exp2(s - m_new).astype(BF16)
            acc_ref[:, c0:] = alpha * acc_ref[:, c0:] + jnp.dot(
                vt, p, preferred_element_type=F32)
            m_old = jnp.concatenate([m_old[:, :c0], m_new], axis=1) if c0 else m_new
        return m_old

    def body(a, m):
        t = 2 * a
        scores(t + 1, s1_ref)
        m = absorb(t, s0_ref, m, diagonal=False)
        scores(t + 2, s0_ref)
        return absorb(t + 1, s1_ref, m, diagonal=False)

    scores(0, s0_ref)
    m = lax.fori_loop(0, i // 2, body, jnp.full((1, qb), NEG, F32))
    odd = lax.rem(i, 2) == 1

    @pl.when(jnp.logical_not(odd))
    def _():
        absorb(i, s0_ref, m, diagonal=True)

    @pl.when(odd)
    def _():
        scores(i, s1_ref)
        m1 = absorb(i - 1, s0_ref, m, diagonal=False)
        absorb(i, s1_ref, m1, diagonal=True)

    l = acc_ref[HEAD_DIM:HEAD_DIM + 1, :]
    o = (acc_ref[0:HEAD_DIM, :] * (1.0 / l)).T
    o_ref[0] = (o * z_ref[0].astype(F32)).astype(o_ref.dtype)


def _fox_attn(q3, k3, v3, ck, zs3, n_heads, qb=1024, kb=512):
    b, s, d = q3.shape
    qspec = pl.BlockSpec((1, qb, HEAD_DIM), lambda bi, h, i: (bi, i, h))
    kvspec = pl.BlockSpec((1, s, HEAD_DIM), lambda bi, h, i: (bi, 0, h))
    return pl.pallas_call(
        functools.partial(_fox_attn_kernel, qb=qb, kb=kb),
        grid=(b, n_heads, s // qb),
        in_specs=[qspec, kvspec, kvspec,
                  pl.BlockSpec((1, s, ck.shape[-1]), lambda bi, h, i: (bi, 0, 0)),
                  qspec],
        out_specs=qspec,
        out_shape=jax.ShapeDtypeStruct((b, s, d), BF16),
        scratch_shapes=[pltpu.VMEM((HEAD_DIM + ONES_ROWS, qb), F32),
                        pltpu.VMEM((qb, qb), F32), pltpu.VMEM((qb, qb), F32),
                        pltpu.VMEM((s, 2 * HEAD_DIM), BF16),
                        pltpu.VMEM((s // kb, HEAD_DIM + ONES_ROWS, kb), BF16)],
        compiler_params=_cparams(("parallel", "parallel", "arbitrary")),
        name="fox_attn",
    )(q3, k3, v3, ck, zs3)


def _hgrn_kernel(q_ref, k_ref, g_ref, v_ref, z_ref, go_ref, o_ref, st_ref, *, tt, hb):
    c_len = HG_CHUNK

    @pl.when(pl.program_id(2) == 0)
    def _():
        st_ref[...] = jnp.zeros_like(st_ref)

    n_tiles = c_len // SUBLANES
    sub = lax.broadcasted_iota(jnp.int32, (SUBLANES, HEAD_DIM), 0)
    ti = lax.broadcasted_iota(jnp.int32, (c_len, c_len), 0)
    si = lax.broadcasted_iota(jnp.int32, (c_len, c_len), 1)
    txs = ti ^ si
    eye = ti == si
    pair = {n: (ti > si) & (txs >= n) & (txs < 2 * n) for n in HG_LEVELS}
    upper = {n: (sub & n) != 0 for n in HG_LEVELS if n < SUBLANES}
    go = go_ref[...]
    nt_dims = (((1,), (1,)), ((), ()))

    def tiles(x):
        return [x[i * SUBLANES:(i + 1) * SUBLANES, :] for i in range(n_tiles)]

    def decay_products(rows, cols):
        q = q_ref[0, rows, cols].astype(F32)
        k = k_ref[0, rows, cols].astype(F32)
        qt, kt = tiles(q), tiles(k)
        seg = tiles(g_ref[0, rows, cols])
        tot = list(seg)

        diag = jnp.sum(q * k, axis=-1, keepdims=True)
        level_att = []
        for n in HG_LEVELS:
            x = [None] * n_tiles
            if n < SUBLANES:
                up = upper[n]
                for i in range(n_tiles):
                    e = jnp.exp(jnp.where(up, seg[i], tot[i] - seg[i]))
                    x[i] = jnp.where(up, qt[i], kt[i]) * e
                    fwd = pltpu.roll(tot[i], n, axis=0)
                    sib = fwd if 2 * n == SUBLANES else jnp.where(
                        up, fwd, pltpu.roll(tot[i], SUBLANES - n, axis=0))
                    seg[i] = seg[i] + jnp.where(up, sib, 0.0)
                    tot[i] = tot[i] + sib
            else:
                m = n // SUBLANES
                for b0 in range(0, n_tiles, 2 * m):
                    t_lo, t_hi = tot[b0], tot[b0 + m]
                    for i in range(b0, b0 + m):
                        x[i] = kt[i] * jnp.exp(t_lo - seg[i])
                    for i in range(b0 + m, b0 + 2 * m):
                        x[i] = qt[i] * jnp.exp(seg[i])
                        seg[i] = seg[i] + t_lo
                    tot[b0:b0 + 2 * m] = [t_lo + t_hi] * (2 * m)
            xn = jnp.concatenate(x, axis=0).astype(BF16)
            level_att.append(lax.dot_general(xn, xn, nt_dims, preferred_element_type=F32))

        total = tot[0]
        q_in = jnp.concatenate([qt[i] * jnp.exp(seg[i]) for i in range(n_tiles)], axis=0)
        k_out = jnp.concatenate([kt[i] * jnp.exp(total - seg[i]) for i in range(n_tiles)],
                                axis=0)
        return (diag, level_att, q_in.astype(BF16), k_out.astype(BF16),
                jnp.exp(total[0:1, :]))

    def finish(rows, cols, parts, st):
        diag, level_att, q_in, k_out, decay = parts
        v = v_ref[0, rows, cols]
        att = jnp.where(eye, diag, 0.0)
        for n, a_n in zip(HG_LEVELS, level_att):
            att = jnp.where(pair[n], a_n, att)
        intra = jnp.dot(att.astype(BF16), v, preferred_element_type=F32)
        kv = lax.dot_general(v, k_out, (((0,), (0,)), ((), ())),
                             preferred_element_type=F32)
        inter = lax.dot_general(q_in, st.astype(BF16), nt_dims,
                                preferred_element_type=F32)
        y = inter + intra
        y = y * _row_rms_scale(y) * go
        o_ref[0, rows, cols] = (y * z_ref[0, rows, cols].astype(F32)).astype(o_ref.dtype)
        return st * decay + kv

    state = [st_ref[h] for h in range(hb)]
    pending = None
    for ci in range(tt // c_len):
        rows = slice(ci * c_len, (ci + 1) * c_len)
        for h in range(hb):
            cols = slice(h * HEAD_DIM, (h + 1) * HEAD_DIM)
            parts = decay_products(rows, cols)
            if pending is not None:
                p_rows, p_cols, p_h, p_parts = pending
                state[p_h] = finish(p_rows, p_cols, p_parts, state[p_h])
            pending = (rows, cols, h, parts)
    p_rows, p_cols, p_h, p_parts = pending
    state[p_h] = finish(p_rows, p_cols, p_parts, state[p_h])
    for h in range(hb):
        st_ref[h] = state[h]


def _hgrn(q3, k3, g3, v3, zs3, g_o, tt=256, hb=8):
    b, s, d = q3.shape
    n_groups = d // (HEAD_DIM * hb)
    spec = pl.BlockSpec((1, tt, HEAD_DIM * hb), lambda bi, hg, t: (bi, t, hg))
    return pl.pallas_call(
        functools.partial(_hgrn_kernel, tt=tt, hb=hb),
        grid=(b, n_groups, s // tt),
        in_specs=[spec, spec, spec, spec, spec,
                  pl.BlockSpec((1, HEAD_DIM), lambda bi, hg, t: (0, 0))],
        out_specs=spec,
        out_shape=jax.ShapeDtypeStruct((b, s, d), BF16),
        scratch_shapes=[pltpu.VMEM((hb, HEAD_DIM, HEAD_DIM), F32)],
        compiler_params=_cparams(("parallel", "parallel", "arbitrary")),
        name="hgrn2_scan",
    )(q3, k3, g3, v3, zs3, g_o.reshape(1, HEAD_DIM))


def _out_kernel(og_ref, h_ref, p_ref, wo_ref, wpe_ref, wpg_ref, post_ref, *rest, emit_xn):
    if emit_xn:
        pre_ref, h_out_ref, xn_ref = rest
    else:
        (h_out_ref,) = rest
    y = jnp.dot(og_ref[...], wo_ref[...], preferred_element_type=F32)
    h1 = h_ref[...] + y * _row_rms_scale(y) * post_ref[...]
    pe = jnp.dot(p_ref[...].astype(BF16), wpe_ref[...], preferred_element_type=F32)
    gate = _sigmoid(jnp.dot(h1.astype(BF16), wpg_ref[...], preferred_element_type=F32))
    h2 = h1 + pe * gate
    if emit_xn:
        xn_ref[...] = (h2 * _row_rms_scale(h2) * pre_ref[...]).astype(xn_ref.dtype)
    h_out_ref[...] = h2


def _out_block(og, h, p_all, layer, w_out, w_pe, w_pg, post, pre_next, tm=512):
    t, d = h.shape
    pdim = p_all.shape[-1]
    emit_xn = pre_next is not None
    row = lambda i: (i, 0)
    whole = lambda i: (0, 0)
    const = dict(pipeline_mode=pl.Buffered(1))
    in_specs = [pl.BlockSpec((tm, d), row),
                pl.BlockSpec((tm, d), row),
                pl.BlockSpec((None, tm, pdim), lambda i: (layer, i, 0)),
                pl.BlockSpec((d, d), whole, **const),
                pl.BlockSpec((pdim, d), whole, **const),
                pl.BlockSpec((d, d), whole, **const),
                pl.BlockSpec((1, d), whole)]
    args = [og, h, p_all, w_out, w_pe, w_pg, post.reshape(1, d)]
    out_specs = [pl.BlockSpec((tm, d), row)]
    out_shape = [jax.ShapeDtypeStruct((t, d), F32)]
    if emit_xn:
        in_specs.append(pl.BlockSpec((1, d), whole))
        args.append(pre_next.reshape(1, d))
        out_specs.append(pl.BlockSpec((tm, d), row))
        out_shape.append(jax.ShapeDtypeStruct((t, d), BF16))
    outs = pl.pallas_call(
        functools.partial(_out_kernel, emit_xn=emit_xn),
        grid=(t // tm,),
        in_specs=in_specs,
        out_specs=out_specs,
        out_shape=out_shape,
        compiler_params=_cparams(("parallel",)),
        name=f"out_block_{layer}",
    )(*args)
    return outs if emit_xn else (outs[0], None)


def _fox_layer(xn, b, s, w_in, b_f, g_q, g_k):
    t, d = xn.shape
    n_heads = d // HEAD_DIM
    w = w_in
    scale = LOG2E / math.sqrt(HEAD_DIM)
    gq = g_q.reshape(1, HEAD_DIM)
    gk = g_k.reshape(1, HEAD_DIM)
    (q,) = _proj(xn, w, 0, d, [gq], [_head_spec()], [BF16],
                 functools.partial(_ep_headnorm, scale=scale), "fox_proj_q")
    (k,) = _proj(xn, w, 1, d, [gk], [_head_spec()], [BF16],
                 functools.partial(_ep_headnorm, scale=1.0), "fox_proj_k")
    (v,) = _proj(xn, w, 2, d, [], [], [BF16], _ep_cast, "fox_proj_v")
    (zs,) = _proj(xn, w, 3, d, [], [], [BF16], _ep_silu, "fox_proj_z")

    lanes = HEAD_DIM
    w_f = jnp.zeros((d, lanes), BF16).at[:, :n_heads].set(w[:, 4 * d:].astype(BF16))
    b_fp = jnp.zeros((1, lanes), F32).at[0, :n_heads].set(b_f)
    ck = _fgate(xn.reshape(b, s, d), w_f, b_fp, n_heads)

    r3 = lambda a: a.reshape(b, s, d)
    og = _fox_attn(r3(q), r3(k), r3(v), ck, r3(zs), n_heads)
    return og.reshape(t, d)


def _hgrn_layer(xn, b, s, layer, w_in, lb_logits, g_o):
    t, d = xn.shape
    w = w_in
    (q,) = _proj(xn, w, 0, d, [], [], [BF16], _ep_silu, "hg_proj_q")
    depth = lb_logits.shape[0]
    g, k = _proj(xn, w, 1, d, [lb_logits],
                 [pl.BlockSpec((depth, PROJ_TN), lambda j, i: (0, j))], [F32, BF16],
                 functools.partial(_ep_forget, layer=layer), "hg_proj_f")
    (v,) = _proj(xn, w, 2, d, [], [], [BF16], _ep_cast, "hg_proj_i")
    (zs,) = _proj(xn, w, 3, d, [], [], [BF16], _ep_silu, "hg_proj_z")
    r3 = lambda a: a.reshape(b, s, d)
    og = _hgrn(r3(q), r3(k), r3(g), r3(v), r3(zs), g_o)
    return og.reshape(t, d)


def kernel(x, p, w_in_fox, b_f_fox, g_q_fox, g_k_fox, w_out_fox, w_in_hg, lb_logits,
           g_o_hg, w_out_hg, pre_norm, post_norm, w_pe, w_pg):
    b, s, d = x.shape
    depth = p.shape[0]
    t = b * s
    h = x.reshape(t, d)
    p_all = p.reshape(depth, t, p.shape[-1])
    xn = _prenorm(h, pre_norm[0])
    for i in range(depth):
        j = i // 2
        if i % 2 == 0:
            og = _fox_layer(xn, b, s, w_in_fox[j], b_f_fox[j], g_q_fox[j], g_k_fox[j])
            w_out = w_out_fox[j]
        else:
            og = _hgrn_layer(xn, b, s, i, w_in_hg[j], lb_logits, g_o_hg[j])
            w_out = w_out_hg[j]
        pre_next = pre_norm[i + 1] if i + 1 < depth else None
        h, xn = _out_block(og, h, p_all, i, w_out.astype(BF16), w_pe[i].astype(BF16),
                           w_pg[i].astype(BF16), post_norm[i], pre_next)
    return h.reshape(b, s, d)
```

```python
import functools
import math

import jax
import jax.numpy as jnp
from jax import lax
from jax.experimental import pallas as pl
from jax.experimental.pallas import tpu as pltpu

HEAD_DIM = 128
EPS = 1e-6
NEG = -1e30
LOG2E = math.log2(math.e)
ONES_ROWS = 16
HG_CHUNK = 64
HG_LEVELS = (1, 2, 4, 8, 16, 32)
SUBLANES = 8

F32 = jnp.float32
BF16 = jnp.bfloat16

VMEM_LIMIT_BYTES = 56 * 1024 * 1024


def _cparams(semantics):
    return pltpu.CompilerParams(dimension_semantics=semantics,
                                vmem_limit_bytes=VMEM_LIMIT_BYTES)


def _sigmoid(x):
    return 1.0 / (1.0 + jnp.exp(-x))


def _silu(x):
    return x * _sigmoid(x)


def _row_rms_scale(y):
    return lax.rsqrt(jnp.mean(y * y, axis=-1, keepdims=True) + EPS)


def _prenorm_kernel(x_ref, g_ref, o_ref):
    x = x_ref[...]
    o_ref[...] = (x * _row_rms_scale(x) * g_ref[...]).astype(o_ref.dtype)


def _prenorm(x2d, gain, tm=512):
    t, d = x2d.shape
    return pl.pallas_call(
        _prenorm_kernel,
        grid=(t // tm,),
        in_specs=[pl.BlockSpec((tm, d), lambda i: (i, 0)),
                  pl.BlockSpec((1, d), lambda i: (0, 0))],
        out_specs=pl.BlockSpec((tm, d), lambda i: (i, 0)),
        out_shape=jax.ShapeDtypeStruct((t, d), BF16),
        compiler_params=_cparams(("parallel",)),
        name="prenorm",
    )(x2d, gain.reshape(1, d))


def _proj_kernel(a_ref, w_ref, *rest, epilogue, n_extra, row_parts):
    wb_ref = rest[-1]

    @pl.when(pl.program_id(1) == 0)
    def _():
        wb_ref[...] = w_ref[...].astype(wb_ref.dtype)

    rows = a_ref.shape[0] // row_parts
    parts = [slice(i * rows, (i + 1) * rows) for i in range(row_parts)]
    accs = [jnp.dot(a_ref[r, :], wb_ref[...], preferred_element_type=F32) for r in parts]
    for r, acc in zip(parts, accs):
        epilogue(acc, rest[:n_extra], rest[n_extra:-1], r)


PROJ_ROW_PARTS = 8
PROJ_TM = 1024
PROJ_TN = 1024


def _proj(a, w, seg, n, extras, extra_specs, out_dtypes, epilogue, name,
          tm=PROJ_TM, tn=PROJ_TN):
    t, k = a.shape
    col0 = seg * (n // tn)
    kern = functools.partial(_proj_kernel, epilogue=epilogue, n_extra=len(extras),
                             row_parts=1 if epilogue is _ep_cast else PROJ_ROW_PARTS)
    outs = pl.pallas_call(
        kern,
        grid=(n // tn, t // tm),
        in_specs=[pl.BlockSpec((tm, k), lambda j, i: (i, 0)),
                  pl.BlockSpec((k, tn), lambda j, i: (0, col0 + j))] + list(extra_specs),
        out_specs=[pl.BlockSpec((tm, tn), lambda j, i: (i, j)) for _ in out_dtypes],
        out_shape=[jax.ShapeDtypeStruct((t, n), dt) for dt in out_dtypes],
        scratch_shapes=[pltpu.VMEM((k, tn), a.dtype)],
        compiler_params=_cparams(("parallel", "arbitrary")),
        name=name,
    )(a, w, *extras)
    return outs


def _head_spec():
    return pl.BlockSpec((1, HEAD_DIM), lambda j, i: (0, 0))


def _ep_cast(acc, extras, outs, rows):
    outs[0][rows, :] = acc.astype(outs[0].dtype)


def _ep_silu(acc, extras, outs, rows):
    outs[0][rows, :] = _silu(acc).astype(outs[0].dtype)


def _ep_headnorm(acc, extras, outs, rows, *, scale):
    gain = extras[0][...] * scale
    for h in range(acc.shape[1] // HEAD_DIM):
        cols = slice(h * HEAD_DIM, (h + 1) * HEAD_DIM)
        y = acc[:, cols]
        outs[0][rows, cols] = (y * _row_rms_scale(y) * gain).astype(outs[0].dtype)


def _ep_forget(acc, extras, outs, rows, *, layer):
    logits = extras[0][...]
    e = jnp.exp(logits - jnp.max(logits, axis=0, keepdims=True))
    gam = e / jnp.sum(e, axis=0, keepdims=True)
    lb = jnp.sum(gam[:layer + 1], axis=0, keepdims=True) - gam[0:1]
    forget = lb + (1.0 - lb) * _sigmoid(acc)
    outs[0][rows, :] = jnp.log(forget)
    outs[1][rows, :] = (1.0 - forget).astype(outs[1].dtype)


def _fgate_kernel(a_ref, w_ref, b_ref, c_ref, carry_ref, *, tm, n_heads):
    @pl.when(pl.program_id(1) == 0)
    def _():
        carry_ref[...] = jnp.zeros_like(carry_ref)

    f = jnp.dot(a_ref[0], w_ref[...], preferred_element_type=F32) + b_ref[...]
    x = jnp.minimum(f, 0.0) - jnp.log(1.0 + jnp.exp(-jnp.abs(f)))
    rows = lax.broadcasted_iota(jnp.int32, x.shape, 0)
    d = 1
    while d < tm:
        x = x + jnp.where(rows >= d, pltpu.roll(x, d, axis=0), 0.0)
        d *= 2
    x = x + carry_ref[...]
    carry_ref[...] = x[tm - 1:tm, :]
    pieces = jnp.concatenate(_split3(-x * LOG2E), axis=1)
    lanes = x.shape[1]
    rr = lax.broadcasted_iota(jnp.int32, (3 * lanes, lanes), 0)
    cc = lax.broadcasted_iota(jnp.int32, (3 * lanes, lanes), 1)
    head = rr & (lanes - 1)
    piece = lax.shift_right_logical(rr, lanes.bit_length() - 1)
    place = jnp.where((cc == 3 * head + piece) & (head < n_heads), 1.0, 0.0).astype(BF16)
    c_ref[0] = jnp.dot(pieces, place, preferred_element_type=F32).astype(c_ref.dtype)


def _split3(x):
    hi = x.astype(BF16)
    r1 = x - hi.astype(F32)
    mid = r1.astype(BF16)
    lo = (r1 - mid.astype(F32)).astype(BF16)
    return hi, mid, lo


def _fgate(xn3, w_f, b_f, n_heads, tm=512):
    b, s, d = xn3.shape
    lanes = w_f.shape[1]
    assert 3 * n_heads <= lanes
    return pl.pallas_call(
        functools.partial(_fgate_kernel, tm=tm, n_heads=n_heads),
        grid=(b, s // tm),
        in_specs=[pl.BlockSpec((1, tm, d), lambda bi, i: (bi, i, 0)),
                  pl.BlockSpec((d, lanes), lambda bi, i: (0, 0)),
                  pl.BlockSpec((1, lanes), lambda bi, i: (0, 0))],
        out_specs=pl.BlockSpec((1, tm, lanes), lambda bi, i: (bi, i, 0)),
        out_shape=jax.ShapeDtypeStruct((b, s, lanes), BF16),
        scratch_shapes=[pltpu.VMEM((1, lanes), F32)],
        compiler_params=_cparams(("parallel", "arbitrary")),
        name="fox_fgate",
    )(xn3, w_f, b_f)


def _fox_attn_kernel(q_ref, k_ref, v_ref, ck_ref, z_ref, o_ref, acc_ref, s0_ref, s1_ref,
                     kx_ref, vt_ref, *, qb, kb):
    i = pl.program_id(2)
    per_tile = qb // kb
    seq = k_ref.shape[1]
    lane_q = lax.broadcasted_iota(jnp.int32, (qb, HEAD_DIM), 1)
    q = jnp.concatenate(
        [q_ref[0], jnp.where(lane_q < 3, 1.0, 0.0).astype(BF16)], axis=1)

    @pl.when(i == 0)
    def _():
        n_in = ck_ref.shape[2]
        rr = lax.broadcasted_iota(jnp.int32, (n_in, HEAD_DIM), 0)
        cc = lax.broadcasted_iota(jnp.int32, (n_in, HEAD_DIM), 1)
        pick = jnp.where((cc < 3) & (rr == 3 * pl.program_id(1) + cc),
                         1.0, 0.0).astype(BF16)

        for r0 in range(0, seq, qb):
            ext = jnp.dot(ck_ref[0, r0:r0 + qb, :], pick, preferred_element_type=F32)
            kx_ref[r0:r0 + qb, 0:HEAD_DIM] = k_ref[0, r0:r0 + qb, :]
            kx_ref[r0:r0 + qb, HEAD_DIM:2 * HEAD_DIM] = ext.astype(BF16)
        ones = jnp.ones((ONES_ROWS, kb), BF16)
        for j in range(seq // kb):
            vt_ref[j] = jnp.concatenate([v_ref[0, j * kb:(j + 1) * kb, :].T, ones], axis=0)

    acc_ref[...] = jnp.zeros_like(acc_ref)

    def scores(t, s_ref):
        for u in range(per_tile):
            r0 = pl.multiple_of(t * qb + u * kb, kb)
            s_ref[u * kb:(u + 1) * kb, :] = lax.dot_general(
                kx_ref[pl.ds(r0, kb), :], q, (((1,), (1,)), ((), ())),
                preferred_element_type=F32)

    def absorb(t, s_ref, m_old, diagonal):
        for u in range(per_tile):
            vt = vt_ref[t * per_tile + u]
            c0 = u * kb if diagonal else 0
            s = s_ref[u * kb:(u + 1) * kb, c0:]
            if diagonal:
                tri = (lax.broadcasted_iota(jnp.int32, (kb, kb), 0)
                       <= lax.broadcasted_iota(jnp.int32, (kb, kb), 1))
                parts = [jnp.where(tri, s[:, :kb], NEG)]
                if s.shape[1] > kb:
                    parts.append(s[:, kb:])
                s = jnp.concatenate(parts, axis=1)
            m_part = m_old[:, c0:]
            m_new = jnp.maximum(m_part, jnp.max(s, axis=0, keepdims=True))
            alpha = jnp.exp2(m_part - m_new)
            p = jnp.exp2(s - m_new).astype(BF16)
            acc_ref[:, c0:] = alpha * acc_ref[:, c0:] + jnp.dot(
                vt, p, preferred_element_type=F32)
            m_old = jnp.concatenate([m_old[:, :c0], m_new], axis=1) if c0 else m_new
        return m_old

    def body(a, m):
        t = 2 * a
        scores(t + 1, s1_ref)
        m = absorb(t, s0_ref, m, diagonal=False)
        scores(t + 2, s0_ref)
        return absorb(t + 1, s1_ref, m, diagonal=False)

    scores(0, s0_ref)
    m = lax.fori_loop(0, i // 2, body, jnp.full((1, qb), NEG, F32))
    odd = lax.rem(i, 2) == 1

    @pl.when(jnp.logical_not(odd))
    def _():
        absorb(i, s0_ref, m, diagonal=True)

    @pl.when(odd)
    def _():
        scores(i, s1_ref)
        m1 = absorb(i - 1, s0_ref, m, diagonal=False)
        absorb(i, s1_ref, m1, diagonal=True)

    l = acc_ref[HEAD_DIM:HEAD_DIM + 1, :]
    o = (acc_ref[0:HEAD_DIM, :] * (1.0 / l)).T
    o_ref[0] = (o * z_ref[0].astype(F32)).astype(o_ref.dtype)


def _fox_attn(q3, k3, v3, ck, zs3, n_heads, qb=1024, kb=512):
    b, s, d = q3.shape
    qspec = pl.BlockSpec((1, qb, HEAD_DIM), lambda bi, h, i: (bi, i, h))
    kvspec = pl.BlockSpec((1, s, HEAD_DIM), lambda bi, h, i: (bi, 0, h))
    return pl.pallas_call(
        functools.partial(_fox_attn_kernel, qb=qb, kb=kb),
        grid=(b, n_heads, s // qb),
        in_specs=[qspec, kvspec, kvspec,
                  pl.BlockSpec((1, s, ck.shape[-1]), lambda bi, h, i: (bi, 0, 0)),
                  qspec],
        out_specs=qspec,
        out_shape=jax.ShapeDtypeStruct((b, s, d), BF16),
        scratch_shapes=[pltpu.VMEM((HEAD_DIM + ONES_ROWS, qb), F32),
                        pltpu.VMEM((qb, qb), F32), pltpu.VMEM((qb, qb), F32),
                        pltpu.VMEM((s, 2 * HEAD_DIM), BF16),
                        pltpu.VMEM((s // kb, HEAD_DIM + ONES_ROWS, kb), BF16)],
        compiler_params=_cparams(("parallel", "parallel", "arbitrary")),
        name="fox_attn",
    )(q3, k3, v3, ck, zs3)


def _hgrn_kernel(q_ref, k_ref, g_ref, v_ref, z_ref, go_ref, o_ref, st_ref, *, tt, hb):
    c_len = HG_CHUNK

    @pl.when(pl.program_id(2) == 0)
    def _():
        st_ref[...] = jnp.zeros_like(st_ref)

    n_tiles = c_len // SUBLANES
    sub = lax.broadcasted_iota(jnp.int32, (SUBLANES, HEAD_DIM), 0)
    ti = lax.broadcasted_iota(jnp.int32, (c_len, c_len), 0)
    si = lax.broadcasted_iota(jnp.int32, (c_len, c_len), 1)
    txs = ti ^ si
    eye = ti == si
    pair = {n: (ti > si) & (txs >= n) & (txs < 2 * n) for n in HG_LEVELS}
    upper = {n: (sub & n) != 0 for n in HG_LEVELS if n < SUBLANES}
    go = go_ref[...]
    nt_dims = (((1,), (1,)), ((), ()))

    def tiles(x):
        return [x[i * SUBLANES:(i + 1) * SUBLANES, :] for i in range(n_tiles)]

    def decay_products(rows, cols):
        q = q_ref[0, rows, cols].astype(F32)
        k = k_ref[0, rows, cols].astype(F32)
        qt, kt = tiles(q), tiles(k)
        seg = tiles(g_ref[0, rows, cols])
        tot = list(seg)

        diag = jnp.sum(q * k, axis=-1, keepdims=True)
        level_att = []
        for n in HG_LEVELS:
            x = [None] * n_tiles
            if n < SUBLANES:
                up = upper[n]
                for i in range(n_tiles):
                    e = jnp.exp(jnp.where(up, seg[i], tot[i] - seg[i]))
                    x[i] = jnp.where(up, qt[i], kt[i]) * e
                    fwd = pltpu.roll(tot[i], n, axis=0)
                    sib = fwd if 2 * n == SUBLANES else jnp.where(
                        up, fwd, pltpu.roll(tot[i], SUBLANES - n, axis=0))
                    seg[i] = seg[i] + jnp.where(up, sib, 0.0)
                    tot[i] = tot[i] + sib
            else:
                m = n // SUBLANES
                for b0 in range(0, n_tiles, 2 * m):
                    t_lo, t_hi = tot[b0], tot[b0 + m]
                    for i in range(b0, b0 + m):
                        x[i] = kt[i] * jnp.exp(t_lo - seg[i])
                    for i in range(b0 + m, b0 + 2 * m):
                        x[i] = qt[i] * jnp.exp(seg[i])
                        seg[i] = seg[i] + t_lo
                    tot[b0:b0 + 2 * m] = [t_lo + t_hi] * (2 * m)
            xn = jnp.concatenate(x, axis=0).astype(BF16)
            level_att.append(lax.dot_general(xn, xn, nt_dims, preferred_element_type=F32))

        total = tot[0]
        q_in = jnp.concatenate([qt[i] * jnp.exp(seg[i]) for i in range(n_tiles)], axis=0)
        k_out = jnp.concatenate([kt[i] * jnp.exp(total - seg[i]) for i in range(n_tiles)],
                                axis=0)
        return (diag, level_att, q_in.astype(BF16), k_out.astype(BF16),
                jnp.exp(total[0:1, :]))

    def finish(rows, cols, parts, st):
        diag, level_att, q_in, k_out, decay = parts
        v = v_ref[0, rows, cols]
        att = jnp.where(eye, diag, 0.0)
        for n, a_n in zip(HG_LEVELS, level_att):
            att = jnp.where(pair[n], a_n, att)
        intra = jnp.dot(att.astype(BF16), v, preferred_element_type=F32)
        kv = lax.dot_general(v, k_out, (((0,), (0,)), ((), ())),
                             preferred_element_type=F32)
        inter = lax.dot_general(q_in, st.astype(BF16), nt_dims,
                                preferred_element_type=F32)
        y = inter + intra
        y = y * _row_rms_scale(y) * go
        o_ref[0, rows, cols] = (y * z_ref[0, rows, cols].astype(F32)).astype(o_ref.dtype)
        return st * decay + kv

    state = [st_ref[h] for h in range(hb)]
    pending = None
    for ci in range(tt // c_len):
        rows = slice(ci * c_len, (ci + 1) * c_len)
        for h in range(hb):
            cols = slice(h * HEAD_DIM, (h + 1) * HEAD_DIM)
            parts = decay_products(rows, cols)
            if pending is not None:
                p_rows, p_cols, p_h, p_parts = pending
                state[p_h] = finish(p_rows, p_cols, p_parts, state[p_h])
            pending = (rows, cols, h, parts)
    p_rows, p_cols, p_h, p_parts = pending
    state[p_h] = finish(p_rows, p_cols, p_parts, state[p_h])
    for h in range(hb):
        st_ref[h] = state[h]


def _hgrn(q3, k3, g3, v3, zs3, g_o, tt=256, hb=8):
    b, s, d = q3.shape
    n_groups = d // (HEAD_DIM * hb)
    spec = pl.BlockSpec((1, tt, HEAD_DIM * hb), lambda bi, hg, t: (bi, t, hg))
    return pl.pallas_call(
        functools.partial(_hgrn_kernel, tt=tt, hb=hb),
        grid=(b, n_groups, s // tt),
        in_specs=[spec, spec, spec, spec, spec,
                  pl.BlockSpec((1, HEAD_DIM), lambda bi, hg, t: (0, 0))],
        out_specs=spec,
        out_shape=jax.ShapeDtypeStruct((b, s, d), BF16),
        scratch_shapes=[pltpu.VMEM((hb, HEAD_DIM, HEAD_DIM), F32)],
        compiler_params=_cparams(("parallel", "parallel", "arbitrary")),
        name="hgrn2_scan",
    )(q3, k3, g3, v3, zs3, g_o.reshape(1, HEAD_DIM))


def _out_kernel(og_ref, h_ref, p_ref, wo_ref, wpe_ref, wpg_ref, post_ref, *rest, emit_xn):
    if emit_xn:
        pre_ref, h_out_ref, xn_ref = rest
    else:
        (h_out_ref,) = rest
    y = jnp.dot(og_ref[...], wo_ref[...], preferred_element_type=F32)
    h1 = h_ref[...] + y * _row_rms_scale(y) * post_ref[...]
    pe = jnp.dot(p_ref[...].astype(BF16), wpe_ref[...], preferred_element_type=F32)
    gate = _sigmoid(jnp.dot(h1.astype(BF16), wpg_ref[...], preferred_element_type=F32))
    h2 = h1 + pe * gate
    if emit_xn:
        xn_ref[...] = (h2 * _row_rms_scale(h2) * pre_ref[...]).astype(xn_ref.dtype)
    h_out_ref[...] = h2


def _out_block(og, h, p_all, layer, w_out, w_pe, w_pg, post, pre_next, tm=512):
    t, d = h.shape
    pdim = p_all.shape[-1]
    emit_xn = pre_next is not None
    row = lambda i: (i, 0)
    whole = lambda i: (0, 0)
    const = dict(pipeline_mode=pl.Buffered(1))
    in_specs = [pl.BlockSpec((tm, d), row),
                pl.BlockSpec((tm, d), row),
                pl.BlockSpec((None, tm, pdim), lambda i: (layer, i, 0)),
                pl.BlockSpec((d, d), whole, **const),
                pl.BlockSpec((pdim, d), whole, **const),
                pl.BlockSpec((d, d), whole, **const),
                pl.BlockSpec((1, d), whole)]
    args = [og, h, p_all, w_out, w_pe, w_pg, post.reshape(1, d)]
    out_specs = [pl.BlockSpec((tm, d), row)]
    out_shape = [jax.ShapeDtypeStruct((t, d), F32)]
    if emit_xn:
        in_specs.append(pl.BlockSpec((1, d), whole))
        args.append(pre_next.reshape(1, d))
        out_specs.append(pl.BlockSpec((tm, d), row))
        out_shape.append(jax.ShapeDtypeStruct((t, d), BF16))
    outs = pl.pallas_call(
        functools.partial(_out_kernel, emit_xn=emit_xn),
        grid=(t // tm,),
        in_specs=in_specs,
        out_specs=out_specs,
        out_shape=out_shape,
        compiler_params=_cparams(("parallel",)),
        name=f"out_block_{layer}",
    )(*args)
    return outs if emit_xn else (outs[0], None)


def _fox_layer(xn, b, s, w_in, b_f, g_q, g_k):
    t, d = xn.shape
    n_heads = d // HEAD_DIM
    w = w_in
    scale = LOG2E / math.sqrt(HEAD_DIM)
    gq = g_q.reshape(1, HEAD_DIM)
    gk = g_k.reshape(1, HEAD_DIM)
    (q,) = _proj(xn, w, 0, d, [gq], [_head_spec()], [BF16],
                 functools.partial(_ep_headnorm, scale=scale), "fox_proj_q")
    (k,) = _proj(xn, w, 1, d, [gk], [_head_spec()], [BF16],
                 functools.partial(_ep_headnorm, scale=1.0), "fox_proj_k")
    (v,) = _proj(xn, w, 2, d, [], [], [BF16], _ep_cast, "fox_proj_v")
    (zs,) = _proj(xn, w, 3, d, [], [], [BF16], _ep_silu, "fox_proj_z")

    lanes = HEAD_DIM
    w_f = jnp.zeros((d, lanes), BF16).at[:, :n_heads].set(w[:, 4 * d:].astype(BF16))
    b_fp = jnp.zeros((1, lanes), F32).at[0, :n_heads].set(b_f)
    ck = _fgate(xn.reshape(b, s, d), w_f, b_fp, n_heads)

    r3 = lambda a: a.reshape(b, s, d)
    og = _fox_attn(r3(q), r3(k), r3(v), ck, r3(zs), n_heads)
    return og.reshape(t, d)


def _hgrn_layer(xn, b, s, layer, w_in, lb_logits, g_o):
    t, d = xn.shape
    w = w_in
    (q,) = _proj(xn, w, 0, d, [], [], [BF16], _ep_silu, "hg_proj_q")
    depth = lb_logits.shape[0]
    g, k = _proj(xn, w, 1, d, [lb_logits],
                 [pl.BlockSpec((depth, PROJ_TN), lambda j, i: (0, j))], [F32, BF16],
                 functools.partial(_ep_forget, layer=layer), "hg_proj_f")
    (v,) = _proj(xn, w, 2, d, [], [], [BF16], _ep_cast, "hg_proj_i")
    (zs,) = _proj(xn, w, 3, d, [], [], [BF16], _ep_silu, "hg_proj_z")
    r3 = lambda a: a.reshape(b, s, d)
    og = _hgrn(r3(q), r3(k), r3(g), r3(v), r3(zs), g_o)
    return og.reshape(t, d)


def kernel(x, p, w_in_fox, b_f_fox, g_q_fox, g_k_fox, w_out_fox, w_in_hg, lb_logits,
           g_o_hg, w_out_hg, pre_norm, post_norm, w_pe, w_pg):
    b, s, d = x.shape
    depth = p.shape[0]
    t = b * s
    h = x.reshape(t, d)
    p_all = p.reshape(depth, t, p.shape[-1])
    xn = _prenorm(h, pre_norm[0])
    for i in range(depth):
        j = i // 2
        if i % 2 == 0:
            og = _fox_layer(xn, b, s, w_in_fox[j], b_f_fox[j], g_q_fox[j], g_k_fox[j])
            w_out = w_out_fox[j]
        else:
            og = _hgrn_layer(xn, b, s, i, w_in_hg[j], lb_logits, g_o_hg[j])
            w_out = w_out_hg[j]
        pre_next = pre_norm[i + 1] if i + 1 < depth else None
        h, xn = _out_block(og, h, p_all, i, w_out.astype(BF16), w_pe[i].astype(BF16),
                           w_pg[i].astype(BF16), post_norm[i], pre_next)
    return h.reshape(b, s, d)
```
